```python
import jax, jax.numpy as jnp
from jax import lax
import numpy as np

D_MODEL = 1024
BATCH = 8
SEQ = 4096
DEPTH = 4

CHUNK = 64
Q_BLOCK = 128

D_MIX = D_MODEL
FOX_HEADS = 8
FOX_HEAD_DIM = 64
FOX_W = FOX_HEADS * FOX_HEAD_DIM
MLA_HEADS = 8
MLA_NOPE = 64
MLA_ROPE = 32
MLA_V = 64
MLA_W = MLA_HEADS * MLA_V
Q_LORA = 256
KV_LORA = 128
ROPE_THETA = 10000.0
EPS = 1e-6

IN_SIZES = (FOX_W, FOX_W, FOX_W, FOX_HEADS, FOX_W,
            Q_LORA, KV_LORA, MLA_ROPE, MLA_W)
N_IN = FOX_W * 4 + FOX_HEADS + Q_LORA + KV_LORA + MLA_ROPE + MLA_W

kernel_name = "hybrid_fox_mla_adaln_trunk"


def rms_norm(x, g):
    xf = x.astype(jnp.float32)
    y = xf * lax.rsqrt(jnp.mean(xf * xf, axis=-1, keepdims=True) + EPS) * g.astype(jnp.float32)
    return y.astype(x.dtype)


def to_blocks(a):
    b, s = a.shape[0], a.shape[1]
    return a.reshape((b, s // Q_BLOCK, Q_BLOCK) + a.shape[2:]).swapaxes(0, 1)


def from_blocks(a):
    nb, b, qb = a.shape[0], a.shape[1], a.shape[2]
    return a.swapaxes(0, 1).reshape((b, nb * qb) + a.shape[3:])


def apply_rope(t, cos, sin):
    tf = t.astype(jnp.float32)
    t1, t2 = jnp.split(tf, 2, axis=-1)
    out = jnp.concatenate([t1 * cos - t2 * sin, t2 * cos + t1 * sin], axis=-1)
    return out.astype(t.dtype)


def fox_attention(q, k, v, log_f):
    s_len = q.shape[1]
    cum = jnp.cumsum(log_f, axis=1)
    cum_k = cum.transpose(0, 2, 1)
    key_idx = jnp.arange(s_len)
    scale = FOX_HEAD_DIM ** -0.5

    def block(args):
        qb, cum_qb, start = args
        s = jnp.einsum('bqhd,bkhd->bhqk', qb, k, preferred_element_type=jnp.float32) * scale
        s = s + cum_qb.transpose(0, 2, 1)[..., None] - cum_k[:, :, None, :]
        q_idx = start + jnp.arange(Q_BLOCK)
        mask = key_idx[None, :] <= q_idx[:, None]
        p = jax.nn.softmax(jnp.where(mask, s, -jnp.inf), axis=-1)
        return jnp.einsum('bhqk,bkhd->bqhd', p.astype(v.dtype), v)

    starts = jnp.arange(s_len // Q_BLOCK, dtype=jnp.int32) * Q_BLOCK
    out = lax.map(block, (to_blocks(q), to_blocks(cum), starts))
    return from_blocks(out)


def mla_attention(q_nope, q_rope, k_nope, k_rope, v):
    s_len = q_nope.shape[1]
    key_chunk = jnp.arange(s_len) // CHUNK
    scale = (MLA_NOPE + MLA_ROPE) ** -0.5

    def block(args):
        qn, qr, start = args
        s = jnp.einsum('bqhd,bkhd->bhqk', qn, k_nope, preferred_element_type=jnp.float32)
        s = s + jnp.einsum('bqhr,bkr->bhqk', qr, k_rope, preferred_element_type=jnp.float32)
        q_chunk = (start + jnp.arange(Q_BLOCK)) // CHUNK
        mask = key_chunk[None, :] <= q_chunk[:, None]
        p = jax.nn.softmax(jnp.where(mask, s * scale, -jnp.inf), axis=-1)
        return jnp.einsum('bhqk,bkhd->bqhd', p.astype(v.dtype), v)

    starts = jnp.arange(s_len // Q_BLOCK, dtype=jnp.int32) * Q_BLOCK
    out = lax.map(block, (to_blocks(q_nope), to_blocks(q_rope), starts))
    return from_blocks(out)


def setup_inputs(seed: int = 0) -> dict:
    key = jax.random.key(seed)
    ks = jax.random.split(key, 16)
    f32 = jnp.float32
    x = jax.random.normal(ks[0], (BATCH, SEQ, D_MODEL), f32)
    c = jax.random.normal(ks[1], (BATCH, D_MODEL), f32)
    offset = jax.random.randint(ks[2], (BATCH,), 0, 16, dtype=jnp.int32) * CHUNK
    positions = (offset[:, None] + jnp.arange(SEQ, dtype=jnp.int32)[None, :]).astype(jnp.int32)
    norm_g = 1.0 + 0.02 * jax.random.normal(ks[3], (DEPTH, D_MODEL), f32)
    w_ada = 0.5 * jax.random.normal(ks[4], (DEPTH, D_MODEL, 3 * D_MODEL), f32) * D_MODEL ** -0.5
    b_ada = 0.02 * jax.random.normal(ks[5], (DEPTH, 3 * D_MODEL), f32)
    w_in = jax.random.normal(ks[6], (DEPTH, D_MODEL, N_IN), f32) * D_MODEL ** -0.5
    b_f = jax.random.uniform(ks[7], (DEPTH, FOX_HEADS), f32, 1.0, 4.0)
    q_norm_g = 1.0 + 0.02 * jax.random.normal(ks[8], (DEPTH, Q_LORA), f32)
    w_uq = jax.random.normal(ks[9], (DEPTH, Q_LORA, MLA_HEADS * (MLA_NOPE + MLA_ROPE)), f32) * Q_LORA ** -0.5
    kv_norm_g = 1.0 + 0.02 * jax.random.normal(ks[10], (DEPTH, KV_LORA), f32)
    w_ukv = jax.random.normal(ks[11], (DEPTH, KV_LORA, MLA_HEADS * (MLA_NOPE + MLA_V)), f32) * KV_LORA ** -0.5
    w_out = jax.random.normal(ks[12], (DEPTH, D_MIX, D_MODEL), f32) * D_MIX ** -0.5
    final_g = 1.0 + 0.02 * jax.random.normal(ks[13], (D_MODEL,), f32)
    return {"x": x, "c": c, "positions": positions, "norm_g": norm_g, "w_ada": w_ada,
            "b_ada": b_ada, "w_in": w_in, "b_f": b_f, "q_norm_g": q_norm_g, "w_uq": w_uq,
            "kv_norm_g": kv_norm_g, "w_ukv": w_ukv, "w_out": w_out, "final_g": final_g}


def reference(x, c, positions, norm_g, w_ada, b_ada, w_in, b_f, q_norm_g, w_uq,
              kv_norm_g, w_ukv, w_out, final_g):
    b, s_len, _ = x.shape
    splits = [int(i) for i in np.cumsum(IN_SIZES)[:-1]]

    inv_freq = 1.0 / (ROPE_THETA ** (jnp.arange(0, MLA_ROPE, 2, dtype=jnp.float32) / MLA_ROPE))
    ang = positions.astype(jnp.float32)[..., None] * inv_freq
    cos, sin = jnp.cos(ang), jnp.sin(ang)

    c_act = jax.nn.silu(c)
    for l in range(DEPTH):
        mod = c_act @ w_ada[l] + b_ada[l]
        shift, scale, gate = jnp.split(mod, 3, axis=-1)
        h = rms_norm(x, norm_g[l]) * (1.0 + scale[:, None, :]) + shift[:, None, :]

        z = h @ w_in[l]
        fq, fk, fv, ff, fg, q_lat, kv_lat, k_r, mg = jnp.split(z, splits, axis=-1)

        log_f = jax.nn.log_sigmoid(ff.astype(jnp.float32) + b_f[l].astype(jnp.float32))
        y_fox = fox_attention(fq.reshape(b, s_len, FOX_HEADS, FOX_HEAD_DIM),
                              fk.reshape(b, s_len, FOX_HEADS, FOX_HEAD_DIM),
                              fv.reshape(b, s_len, FOX_HEADS, FOX_HEAD_DIM), log_f)
        y_fox = y_fox.reshape(b, s_len, FOX_W) * jax.nn.silu(fg)

        q = (rms_norm(q_lat, q_norm_g[l]) @ w_uq[l]).reshape(b, s_len, MLA_HEADS, MLA_NOPE + MLA_ROPE)
        q_nope, q_rope = q[..., :MLA_NOPE], q[..., MLA_NOPE:]
        q_rope = apply_rope(q_rope, cos[:, :, None, :], sin[:, :, None, :])
        kv = (rms_norm(kv_lat, kv_norm_g[l]) @ w_ukv[l]).reshape(b, s_len, MLA_HEADS, MLA_NOPE + MLA_V)
        k_nope, v = kv[..., :MLA_NOPE], kv[..., MLA_NOPE:]
        k_rope = apply_rope(k_r, cos, sin)
        y_mla = mla_attention(q_nope, q_rope, k_nope, k_rope, v)
        y_mla = y_mla.reshape(b, s_len, MLA_W) * jax.nn.silu(mg)

        y = jnp.concatenate([y_fox, y_mla], axis=-1) @ w_out[l]
        x = x + gate[:, None, :] * y

    return rms_norm(x, final_g)
```

```python
import functools
import math

import numpy as np
import jax
import jax.numpy as jnp
from jax import lax
from jax.experimental import pallas as pl
from jax.experimental.pallas import tpu as pltpu

D_MODEL = 1024
DEPTH = 4
CHUNK = 64
HEADS = 8
HEAD_DIM = 64
FOX_W = HEADS * HEAD_DIM
MLA_ROPE = 32
MLA_QK = HEAD_DIM + MLA_ROPE
Q_LORA = 256
KV_LORA = 128
ROPE_THETA = 10000.0
EPS = 1e-6

LANES = 128
HEAD_BLOCKS = HEADS * LANES
LOG2E = 1.4426950408889634
NEG_BIG = -1e30

SEQ_TILE = 512
Q_TILE = 512
VMEM_LIMIT = 56 * 1024 * 1024

AUG0 = HEAD_DIM

R_FF, R_FG, R_QL, R_KVL, R_KR, R_KRR, R_MG, R_END = 0, 128, 640, 896, 1024, 1152, 1280, 1792

F32 = jnp.float32
BF16 = jnp.bfloat16


def _dot(a, b):
    return jnp.dot(a, b, preferred_element_type=F32)


def _split3(v):
    hi = v.astype(BF16)
    r1 = v - hi.astype(F32)
    mid = r1.astype(BF16)
    lo = (r1 - mid.astype(F32)).astype(BF16)
    return hi, mid, lo


def _adaln_kernel(c_ref, w_ref, b_ref, o_ref):
    cv = c_ref[...]
    ca = (cv * jax.nn.sigmoid(cv)).astype(BF16)
    o_ref[0] = _dot(ca, w_ref[0].astype(BF16)) + b_ref[0]


def _adaln(c, w_ada, b_ada):
    batch = c.shape[0]
    n_tiles = 3
    return pl.pallas_call(
        _adaln_kernel,
        grid=(DEPTH, n_tiles),
        in_specs=[
            pl.BlockSpec((batch, D_MODEL), lambda l, n: (0, 0)),
            pl.BlockSpec((1, D_MODEL, D_MODEL), lambda l, n: (l, 0, n)),
            pl.BlockSpec((1, 1, D_MODEL), lambda l, n: (l, 0, n)),
        ],
        out_specs=pl.BlockSpec((1, batch, D_MODEL), lambda l, n: (l, 0, n)),
        out_shape=jax.ShapeDtypeStruct((DEPTH, batch, 3 * D_MODEL), F32),
        compiler_params=pltpu.CompilerParams(
            dimension_semantics=("arbitrary", "arbitrary"), vmem_limit_bytes=VMEM_LIMIT),
        name="adaln_mod",
    )(c, w_ada, b_ada.reshape(DEPTH, 1, 3 * D_MODEL))


def _rope_table_kernel(pos_ref, invf_ref, cos_ref, sin_ref):
    ang = pos_ref[0].astype(F32) * invf_ref[...]
    cos_ref[0] = jnp.cos(ang)
    sin_ref[0] = jnp.sin(ang)


def _rope_tables(positions):
    batch, seq = positions.shape
    inv_freq = 1.0 / (ROPE_THETA ** (jnp.arange(0, MLA_ROPE, 2, dtype=F32) / MLA_ROPE))
    half = MLA_ROPE // 2
    invf_row = jnp.zeros((1, LANES), F32)
    invf_row = invf_row.at[0, HEAD_DIM:HEAD_DIM + half].set(inv_freq)
    invf_row = invf_row.at[0, HEAD_DIM + half:HEAD_DIM + MLA_ROPE].set(inv_freq)
    ts = SEQ_TILE
    out = jax.ShapeDtypeStruct((batch, seq, LANES), F32)
    return pl.pallas_call(
        _rope_table_kernel,
        grid=(batch, seq // ts),
        in_specs=[
            pl.BlockSpec((1, ts, 1), lambda b, s: (b, s, 0)),
            pl.BlockSpec((1, LANES), lambda b, s: (0, 0)),
        ],
        out_specs=[pl.BlockSpec((1, ts, LANES), lambda b, s: (b, s, 0))] * 2,
        out_shape=[out, out],
        compiler_params=pltpu.CompilerParams(
            dimension_semantics=("arbitrary", "arbitrary"), vmem_limit_bytes=VMEM_LIMIT),
        name="rope_tables",
    )(positions.reshape(batch, seq, 1), invf_row)


def _head_blocks(w, width, odd_offset=0):
    k = w.shape[0]
    w3 = w.reshape(k, HEADS, width)
    even = jnp.pad(w3, ((0, 0), (0, 0), (0, LANES - width)))
    if odd_offset:
        odd = jnp.pad(w3, ((0, 0), (0, 0), (odd_offset, LANES - width - odd_offset)))
        is_odd = (jnp.arange(HEADS) % 2 == 1)[None, :, None]
        even = jnp.where(is_odd, odd, even)
    return even.reshape(k, HEADS * LANES)


def _rot_half_cols(w):
    half = MLA_ROPE // 2
    return jnp.concatenate([-w[..., half:], w[..., :half]], axis=-1)


def _const_rows():
    rows = np.zeros((8, HEAD_BLOCKS), np.float32)
    for h in range(HEADS):
        base = h * LANES
        rows[0, base + AUG0 + 3:base + AUG0 + 6] = 1.0
        rows[1, base + AUG0:base + AUG0 + 3] = 1.0
        rows[2, base + (HEAD_DIM if h % 2 == 0 else 0)] = 1.0
    return rows


def _aug_matrices():
    eq = np.zeros((LANES, HEAD_BLOCKS), np.float32)
    ek = np.zeros((LANES, HEAD_BLOCKS), np.float32)
    for p in range(3):
        for h in range(HEADS):
            eq[p * HEADS + h, h * LANES + AUG0 + p] = 1.0
            ek[p * HEADS + h, h * LANES + AUG0 + 3 + p] = -1.0
    return eq, ek


def _prep_layer(w_in_l, w_uq_l, w_ukv_l, w_out_l, b_f_l, q_norm_g_l, kv_norm_g_l):
    o = np.cumsum([0, FOX_W, FOX_W, FOX_W, HEADS, FOX_W, Q_LORA, KV_LORA, MLA_ROPE, FOX_W])
    w_fq, w_fk, w_fv, w_ff, w_fg, w_ql, w_kvl, w_kr, w_mg = (
        w_in_l[:, o[i]:o[i + 1]] for i in range(9))
    wq = _head_blocks(w_fq, HEAD_DIM).astype(BF16)
    wk = _head_blocks(w_fk, HEAD_DIM).astype(BF16)
    wv = _head_blocks(w_fv, HEAD_DIM, odd_offset=HEAD_DIM).astype(BF16)
    pad_ff = jnp.pad(w_ff, ((0, 0), (0, LANES - HEADS)))
    kr_blk = jnp.pad(w_kr, ((0, 0), (HEAD_DIM, LANES - MLA_QK)))
    krr_blk = jnp.pad(_rot_half_cols(w_kr), ((0, 0), (HEAD_DIM, LANES - MLA_QK)))
    wr = jnp.concatenate([pad_ff, w_fg, w_ql, w_kvl, kr_blk, krr_blk, w_mg], axis=1).astype(BF16)

    uq3 = w_uq_l.reshape(Q_LORA, HEADS, MLA_QK)
    wuq = jnp.pad(uq3, ((0, 0), (0, 0), (0, LANES - MLA_QK))).reshape(Q_LORA, HEAD_BLOCKS).astype(BF16)
    uq_rot = jnp.pad(_rot_half_cols(uq3[..., HEAD_DIM:]), ((0, 0), (0, 0), (HEAD_DIM, LANES - MLA_QK)))
    wuqr = uq_rot.reshape(Q_LORA, HEAD_BLOCKS).astype(BF16)
    ukv3 = w_ukv_l.reshape(KV_LORA, HEADS, 2 * HEAD_DIM)
    wkk = _head_blocks(ukv3[..., :HEAD_DIM].reshape(KV_LORA, FOX_W), HEAD_DIM).astype(BF16)
    wkv = _head_blocks(ukv3[..., HEAD_DIM:].reshape(KV_LORA, FOX_W), HEAD_DIM, odd_offset=HEAD_DIM).astype(BF16)

    rows = jnp.asarray(_const_rows())
    rows = rows.at[3, :HEADS].set(b_f_l)
    rows = rows.at[4, :Q_LORA].set(q_norm_g_l)
    rows = rows.at[5, :KV_LORA].set(kv_norm_g_l)
    return dict(wq=wq, wk=wk, wv=wv, wr=wr, wuq=wuq, wuqr=wuqr, wkk=wkk, wkv=wkv, rows=rows,
                wout=w_out_l.astype(BF16))


def _rms(v, g):
    return v * lax.rsqrt(jnp.mean(v * v, axis=-1, keepdims=True) + EPS) * g


def _silu(v):
    return v * jax.nn.sigmoid(v)


def _proj_kernel(x_ref, mod_ref, g_ref, wq_ref, wk_ref, wv_ref, wr_ref, wuq_ref, wuqr_ref,
                 wkk_ref, wkv_ref, eq_ref, ek_ref, rows_ref, ltri_ref, cos_ref, sin_ref,
                 qf_ref, kf_ref, vf_ref, sgf_ref, qm_ref, km_ref, vm_ref, sgm_ref, carry_ref):
    ts = x_ref.shape[1]

    @pl.when(pl.program_id(1) == 0)
    def _():
        carry_ref[...] = jnp.zeros_like(carry_ref)

    xs = x_ref[0]
    shift = mod_ref[0, :, 0:D_MODEL]
    scale = mod_ref[0, :, D_MODEL:2 * D_MODEL]
    h = _rms(xs, g_ref[...]) * (1.0 + scale) + shift
    hb = h.astype(BF16)

    zr = _dot(hb, wr_ref[...])

    ff = zr[:, R_FF:R_FF + LANES] + rows_ref[3:4, 0:LANES]
    lf = jnp.minimum(ff, 0.0) - jnp.log1p(jnp.exp(-jnp.abs(ff)))
    hi, mid, lo = _split3(lf)
    ltri = ltri_ref[...]
    cum = _dot(ltri, hi) + _dot(ltri, mid) + _dot(ltri, lo) + carry_ref[0:1, :]
    carry_ref[...] = jnp.broadcast_to(cum[ts - 1:ts, :], carry_ref.shape)
    chi, cmid, clo = _split3(cum * LOG2E)
    lane = lax.broadcasted_iota(jnp.int32, (ts, LANES), 1)
    cpieces = jnp.where(
        lane < HEADS, chi.astype(F32),
        jnp.where(lane < 2 * HEADS, pltpu.roll(cmid.astype(F32), HEADS, 1),
                  jnp.where(lane < 3 * HEADS, pltpu.roll(clo.astype(F32), 2 * HEADS, 1), 0.0)))
    cb = cpieces.astype(BF16)

    q_scale = HEAD_DIM ** -0.5 * LOG2E
    qf_ref[0] = (_dot(hb, wq_ref[...]) * q_scale + _dot(cb, eq_ref[...]) + rows_ref[0:1, :]).astype(BF16)
    kf_ref[0] = (_dot(hb, wk_ref[...]) + _dot(cb, ek_ref[...]) + rows_ref[1:2, :]).astype(BF16)
    vf_ref[0] = (_dot(hb, wv_ref[...]) + rows_ref[2:3, :]).astype(BF16)
    sgf_ref[0] = _silu(zr[:, R_FG:R_FG + FOX_W]).astype(BF16)
    sgm_ref[0] = _silu(zr[:, R_MG:R_MG + FOX_W]).astype(BF16)

    cos = cos_ref[0]
    sin = sin_ref[0]
    cos8 = jnp.concatenate([cos] * HEADS, axis=1)
    sin8 = jnp.concatenate([sin] * HEADS, axis=1)
    qn = _rms(zr[:, R_QL:R_QL + Q_LORA], rows_ref[4:5, 0:Q_LORA]).astype(BF16)
    m_scale = MLA_QK ** -0.5 * LOG2E
    qm = (_dot(qn, wuq_ref[...]) * cos8 + _dot(qn, wuqr_ref[...]) * sin8) * m_scale
    qm_ref[0] = qm.astype(BF16)
    kvn = _rms(zr[:, R_KVL:R_KVL + KV_LORA], rows_ref[5:6, 0:KV_LORA]).astype(BF16)
    krope = zr[:, R_KR:R_KR + LANES] * cos + zr[:, R_KRR:R_KRR + LANES] * sin
    km_ref[0] = (_dot(kvn, wkk_ref[...]) + jnp.concatenate([krope] * HEADS, axis=1)).astype(BF16)
    vm_ref[0] = (_dot(kvn, wkv_ref[...]) + rows_ref[2:3, :]).astype(BF16)


def _resident(shape):
    nd = len(shape)
    return pl.BlockSpec(shape, lambda b, s: (0,) * nd, pipeline_mode=pl.Buffered(1))


def _project(x, mod_l, norm_g_l, lw, consts, cos_t, sin_t):
    batch, seq, _ = x.shape
    ts = SEQ_TILE
    eq, ek, ltri = consts
    row_tile = lambda w: pl.BlockSpec((1, ts, w), lambda b, s: (b, s, 0))
    wide = jax.ShapeDtypeStruct((batch, seq, HEAD_BLOCKS), BF16)
    narrow = jax.ShapeDtypeStruct((batch, seq, FOX_W), BF16)
    weights = [lw["wq"], lw["wk"], lw["wv"], lw["wr"], lw["wuq"], lw["wuqr"], lw["wkk"], lw["wkv"],
               eq, ek, lw["rows"], ltri]
    return pl.pallas_call(
        _proj_kernel,
        grid=(batch, seq // ts),
        in_specs=[row_tile(D_MODEL),
                  pl.BlockSpec((1, 1, 3 * D_MODEL), lambda b, s: (b, 0, 0)),
                  _resident((1, D_MODEL))]
                 + [_resident(w.shape) for w in weights]
                 + [row_tile(LANES), row_tile(LANES)],
        out_specs=[row_tile(HEAD_BLOCKS)] * 3 + [row_tile(FOX_W)]
                  + [row_tile(HEAD_BLOCKS)] * 3 + [row_tile(FOX_W)],
        out_shape=[wide, wide, wide, narrow, wide, wide, wide, narrow],
        scratch_shapes=[pltpu.VMEM((8, LANES), F32)],
        compiler_params=pltpu.CompilerParams(
            dimension_semantics=("arbitrary", "arbitrary"), vmem_limit_bytes=VMEM_LIMIT),
        name="in_proj",
    )(x, mod_l.reshape(batch, 1, 3 * D_MODEL), norm_g_l.reshape(1, D_MODEL), *weights, cos_t, sin_t)


def _attn_kernel(q_ref, k_ref, v_ref, sg_ref, o_ref, m_ref, acc_ref, *, chunk):
    tq = q_ref.shape[1]
    i = pl.program_id(1)
    m_ref[...] = jnp.full(m_ref.shape, NEG_BIG, F32)
    acc_ref[...] = jnp.zeros(acc_ref.shape, F32)

    row = lax.broadcasted_iota(jnp.int32, (tq, tq), 0)
    col = lax.broadcasted_iota(jnp.int32, (tq, tq), 1)
    if chunk == 1:
        diag_mask = col <= row
    else:
        diag_mask = (col // chunk) <= (row // chunk)

    def kv_step(j, masked):
        start = pl.multiple_of(j * tq, tq)
        for h in range(HEADS):
            blk = slice(h * LANES, (h + 1) * LANES)
            q = q_ref[0, :, blk]
            k = k_ref[0, pl.ds(start, tq), blk]
            s = lax.dot_general(q, k, (((1,), (1,)), ((), ())), preferred_element_type=F32)
            if masked:
                s = jnp.where(diag_mask, s, NEG_BIG)
            m_prev = m_ref[h]
            m_new = jnp.maximum(m_prev, jnp.max(s, axis=1, keepdims=True))
            alpha = jnp.exp2(m_prev - m_new)
            p = jnp.exp2(s - pltpu.repeat(m_new, tq // LANES, 1))
            v = v_ref[0, pl.ds(start, tq), blk]
            acc_ref[h] = acc_ref[h] * alpha + _dot(p.astype(BF16), v)
            m_ref[h] = m_new

    def body(j, carry):
        kv_step(j, False)
        return carry

    lax.fori_loop(0, i, body, 0)
    kv_step(i, True)

    lane = lax.broadcasted_iota(jnp.int32, (tq, LANES), 1)
    for pair in range(HEADS // 2):
        a_even = acc_ref[2 * pair]
        a_odd = acc_ref[2 * pair + 1]
        out_even = a_even / a_even[:, HEAD_DIM:HEAD_DIM + 1]
        out_odd = a_odd / a_odd[:, 0:1]
        out = jnp.where(lane < HEAD_DIM, out_even, out_odd)
        sg = sg_ref[0, :, pair * LANES:(pair + 1) * LANES].astype(F32)
        o_ref[0, :, pair * LANES:(pair + 1) * LANES] = (out * sg).astype(o_ref.dtype)


def _attention(q, k, v, sg, chunk):
    batch, seq, _ = q.shape
    tq = Q_TILE
    return pl.pallas_call(
        functools.partial(_attn_kernel, chunk=chunk),
        grid=(batch, seq // tq),
        in_specs=[
            pl.BlockSpec((1, tq, HEAD_BLOCKS), lambda b, i: (b, i, 0)),
            pl.BlockSpec((1, seq, HEAD_BLOCKS), lambda b, i: (b, 0, 0)),
            pl.BlockSpec((1, seq, HEAD_BLOCKS), lambda b, i: (b, 0, 0)),
            pl.BlockSpec((1, tq, FOX_W), lambda b, i: (b, i, 0)),
        ],
        out_specs=pl.BlockSpec((1, tq, FOX_W), lambda b, i: (b, i, 0)),
        out_shape=jax.ShapeDtypeStruct((batch, seq, FOX_W), BF16),
        scratch_shapes=[pltpu.VMEM((HEADS, tq, LANES), F32), pltpu.VMEM((HEADS, tq, LANES), F32)],
        compiler_params=pltpu.CompilerParams(
            dimension_semantics=("arbitrary", "arbitrary"), vmem_limit_bytes=VMEM_LIMIT),
        name="attn_fox" if chunk == 1 else "attn_mla",
    )(q, k, v, sg)


def _out_kernel(x_ref, yf_ref, ym_ref, w_ref, mod_ref, fg_ref, o_ref, *, final):
    y = _dot(yf_ref[0], w_ref[0:FOX_W, :]) + _dot(ym_ref[0], w_ref[FOX_W:2 * FOX_W, :])
    gate = mod_ref[0, :, 2 * D_MODEL:3 * D_MODEL]
    xn = x_ref[0] + gate * y
    if final:
        xn = _rms(xn, fg_ref[...])
    o_ref[0] = xn


def _out_project(x, yf, ym, wout, mod_l, final_g, final):
    batch, seq, _ = x.shape
    ts = SEQ_TILE
    row_tile = lambda w: pl.BlockSpec((1, ts, w), lambda b, s: (b, s, 0))
    return pl.pallas_call(
        functools.partial(_out_kernel, final=final),
        grid=(batch, seq // ts),
        in_specs=[row_tile(D_MODEL), row_tile(FOX_W), row_tile(FOX_W),
                  _resident((D_MODEL, D_MODEL)),
                  pl.BlockSpec((1, 1, 3 * D_MODEL), lambda b, s: (b, 0, 0)),
                  _resident((1, D_MODEL))],
        out_specs=row_tile(D_MODEL),
        out_shape=jax.ShapeDtypeStruct(x.shape, F32),
        compiler_params=pltpu.CompilerParams(
            dimension_semantics=("arbitrary", "arbitrary"), vmem_limit_bytes=VMEM_LIMIT),
        name="out_proj_final" if final else "out_proj",
    )(x, yf, ym, wout, mod_l.reshape(batch, 1, 3 * D_MODEL), final_g.reshape(1, D_MODEL))


def kernel(x, c, positions, norm_g, w_ada, b_ada, w_in, b_f, q_norm_g, w_uq, kv_norm_g, w_ukv,
           w_out, final_g):
    mod = _adaln(c, w_ada, b_ada)
    cos_t, sin_t = _rope_tables(positions)
    eq, ek = _aug_matrices()
    ltri = np.tril(np.ones((SEQ_TILE, SEQ_TILE), np.float32))
    consts = (jnp.asarray(eq, BF16), jnp.asarray(ek, BF16), jnp.asarray(ltri, BF16))
    for l in range(DEPTH):
        lw = _prep_layer(w_in[l], w_uq[l], w_ukv[l], w_out[l], b_f[l], q_norm_g[l], kv_norm_g[l])
        qf, kf, vf, sgf, qm, km, vm, sgm = _project(x, mod[l], norm_g[l], lw, consts, cos_t, sin_t)
        yf = _attention(qf, kf, vf, sgf, chunk=1)
        ym = _attention(qm, km, vm, sgm, chunk=CHUNK)
        x = _out_project(x, yf, ym, lw["wout"], mod[l], final_g, final=(l == DEPTH - 1))
    return x
```

```python
import functools
import math

import numpy as np
import jax
import jax.numpy as jnp
from jax import lax
from jax.experimental import pallas as pl
from jax.experimental.pallas import tpu as pltpu

D_MODEL = 1024
DEPTH = 4
CHUNK = 64
HEADS = 8
HEAD_DIM = 64
FOX_W = HEADS * HEAD_DIM
MLA_ROPE = 32
MLA_QK = HEAD_DIM + MLA_ROPE
Q_LORA = 256
KV_LORA = 128
ROPE_THETA = 10000.0
EPS = 1e-6

LANES = 128
HEAD_BLOCKS = HEADS * LANES
LOG2E = 1.4426950408889634
NEG_BIG = -1e30

SEQ_TILE = 512
Q_TILE = 512
VMEM_LIMIT = 56 * 1024 * 1024

AUG0 = HEAD_DIM

R_FF, R_FG, R_QL, R_KVL, R_KR, R_KRR, R_MG, R_END = 0, 128, 640, 896, 1024, 1152, 1280, 1792

F32 = jnp.float32
BF16 = jnp.bfloat16


def _dot(a, b):
    return jnp.dot(a, b, preferred_element_type=F32)


def _split3(v):
    hi = v.astype(BF16)
    r1 = v - hi.astype(F32)
    mid = r1.astype(BF16)
    lo = (r1 - mid.astype(F32)).astype(BF16)
    return hi, mid, lo


def _adaln_kernel(c_ref, w_ref, b_ref, o_ref):
    cv = c_ref[...]
    ca = (cv * jax.nn.sigmoid(cv)).astype(BF16)
    o_ref[0] = _dot(ca, w_ref[0].astype(BF16)) + b_ref[0]


def _adaln(c, w_ada, b_ada):
    batch = c.shape[0]
    n_tiles = 3
    return pl.pallas_call(
        _adaln_kernel,
        grid=(DEPTH, n_tiles),
        in_specs=[
            pl.BlockSpec((batch, D_MODEL), lambda l, n: (0, 0)),
            pl.BlockSpec((1, D_MODEL, D_MODEL), lambda l, n: (l, 0, n)),
            pl.BlockSpec((1, 1, D_MODEL), lambda l, n: (l, 0, n)),
        ],
        out_specs=pl.BlockSpec((1, batch, D_MODEL), lambda l, n: (l, 0, n)),
        out_shape=jax.ShapeDtypeStruct((DEPTH, batch, 3 * D_MODEL), F32),
        compiler_params=pltpu.CompilerParams(
            dimension_semantics=("arbitrary", "arbitrary"), vmem_limit_bytes=VMEM_LIMIT),
        name="adaln_mod",
    )(c, w_ada, b_ada.reshape(DEPTH, 1, 3 * D_MODEL))


def _rope_table_kernel(pos_ref, invf_ref, cos_ref, sin_ref):
    ang = pos_ref[0].astype(F32) * invf_ref[...]
    cos_ref[0] = jnp.cos(ang)
    sin_ref[0] = jnp.sin(ang)


def _rope_tables(positions):
    batch, seq = positions.shape
    inv_freq = 1.0 / (ROPE_THETA ** (jnp.arange(0, MLA_ROPE, 2, dtype=F32) / MLA_ROPE))
    half = MLA_ROPE // 2
    invf_row = jnp.zeros((1, LANES), F32)
    invf_row = invf_row.at[0, HEAD_DIM:HEAD_DIM + half].set(inv_freq)
    invf_row = invf_row.at[0, HEAD_DIM + half:HEAD_DIM + MLA_ROPE].set(inv_freq)
    ts = SEQ_TILE
    out = jax.ShapeDtypeStruct((batch, seq, LANES), F32)
    return pl.pallas_call(
        _rope_table_kernel,
        grid=(batch, seq // ts),
        in_specs=[
            pl.BlockSpec((1, ts, 1), lambda b, s: (b, s, 0)),
            pl.BlockSpec((1, LANES), lambda b, s: (0, 0)),
        ],
        out_specs=[pl.BlockSpec((1, ts, LANES), lambda b, s: (b, s, 0))] * 2,
        out_shape=[out, out],
        compiler_params=pltpu.CompilerParams(
            dimension_semantics=("arbitrary", "arbitrary"), vmem_limit_bytes=VMEM_LIMIT),
        name="rope_tables",
    )(positions.reshape(batch, seq, 1), invf_row)


def _head_blocks(w, width):
    k = w.shape[0]
    w3 = w.reshape(k, HEADS, width)
    return jnp.pad(w3, ((0, 0), (0, 0), (0, LANES - width))).reshape(k, HEADS * LANES)


def _rot_half_cols(w):
    half = MLA_ROPE // 2
    return jnp.concatenate([-w[..., half:], w[..., :half]], axis=-1)


def _const_rows():
    rows = np.zeros((8, HEAD_BLOCKS), np.float32)
    for h in range(HEADS):
        base = h * LANES
        rows[0, base + AUG0 + 3:base + AUG0 + 6] = 1.0
        rows[1, base + AUG0:base + AUG0 + 3] = 1.0
    return rows


def _aug_matrices():
    eq = np.zeros((LANES, HEAD_BLOCKS), np.float32)
    ek = np.zeros((LANES, HEAD_BLOCKS), np.float32)
    for p in range(3):
        for h in range(HEADS):
            eq[p * HEADS + h, h * LANES + AUG0 + p] = 1.0
            ek[p * HEADS + h, h * LANES + AUG0 + 3 + p] = -1.0
    return eq, ek


def _prep_layer(w_in_l, w_uq_l, w_ukv_l, w_out_l, b_f_l, q_norm_g_l, kv_norm_g_l):
    o = np.cumsum([0, FOX_W, FOX_W, FOX_W, HEADS, FOX_W, Q_LORA, KV_LORA, MLA_ROPE, FOX_W])
    w_fq, w_fk, w_fv, w_ff, w_fg, w_ql, w_kvl, w_kr, w_mg = (
        w_in_l[:, o[i]:o[i + 1]] for i in range(9))
    wq = _head_blocks(w_fq, HEAD_DIM).astype(BF16)
    wk = _head_blocks(w_fk, HEAD_DIM).astype(BF16)
    wvt = _head_blocks(w_fv, HEAD_DIM).T.astype(BF16)
    pad_ff = jnp.pad(w_ff, ((0, 0), (0, LANES - HEADS)))
    kr_blk = jnp.pad(w_kr, ((0, 0), (HEAD_DIM, LANES - MLA_QK)))
    krr_blk = jnp.pad(_rot_half_cols(w_kr), ((0, 0), (HEAD_DIM, LANES - MLA_QK)))
    wr = jnp.concatenate([pad_ff, w_fg, w_ql, w_kvl, kr_blk, krr_blk, w_mg], axis=1).astype(BF16)

    uq3 = w_uq_l.reshape(Q_LORA, HEADS, MLA_QK)
    wuq = jnp.pad(uq3, ((0, 0), (0, 0), (0, LANES - MLA_QK))).reshape(Q_LORA, HEAD_BLOCKS).astype(BF16)
    uq_rot = jnp.pad(_rot_half_cols(uq3[..., HEAD_DIM:]), ((0, 0), (0, 0), (HEAD_DIM, LANES - MLA_QK)))
    wuqr = uq_rot.reshape(Q_LORA, HEAD_BLOCKS).astype(BF16)
    ukv3 = w_ukv_l.reshape(KV_LORA, HEADS, 2 * HEAD_DIM)
    wkk = _head_blocks(ukv3[..., :HEAD_DIM].reshape(KV_LORA, FOX_W), HEAD_DIM).astype(BF16)
    wkvt = _head_blocks(ukv3[..., HEAD_DIM:].reshape(KV_LORA, FOX_W), HEAD_DIM).T.astype(BF16)

    rows = jnp.asarray(_const_rows())
    rows = rows.at[3, :HEADS].set(b_f_l)
    rows = rows.at[4, :Q_LORA].set(q_norm_g_l)
    rows = rows.at[5, :KV_LORA].set(kv_norm_g_l)
    return dict(wq=wq, wk=wk, wvt=wvt, wr=wr, wuq=wuq, wuqr=wuqr, wkk=wkk, wkvt=wkvt, rows=rows,
                wout=w_out_l.astype(BF16))


def _rms(v, g):
    return v * lax.rsqrt(jnp.mean(v * v, axis=-1, keepdims=True) + EPS) * g


def _silu(v):
    return v * jax.nn.sigmoid(v)


def _dot_nt(a, b):
    return lax.dot_general(a, b, (((1,), (1,)), ((), ())), preferred_element_type=F32)


def _proj_kernel(x_ref, mod_ref, g_ref, wq_ref, wk_ref, wvt_ref, wr_ref, wuq_ref, wuqr_ref,
                 wkk_ref, wkvt_ref, eq_ref, ek_ref, rows_ref, ltri_ref, cos_ref, sin_ref,
                 qf_ref, kf_ref, vft_ref, sgf_ref, qm_ref, km_ref, vmt_ref, sgm_ref, carry_ref):
    ts = x_ref.shape[1]
    vrow = lax.broadcasted_iota(jnp.int32, (HEAD_BLOCKS, ts), 0)
    ones_rows = jnp.where((vrow & (LANES - 1)) == HEAD_DIM, 1.0, 0.0)

    @pl.when(pl.program_id(1) == 0)
    def _():
        carry_ref[...] = jnp.zeros_like(carry_ref)

    xs = x_ref[0]
    shift = mod_ref[0, :, 0:D_MODEL]
    scale = mod_ref[0, :, D_MODEL:2 * D_MODEL]
    h = _rms(xs, g_ref[...]) * (1.0 + scale) + shift
    hb = h.astype(BF16)

    zr = _dot(hb, wr_ref[...])

    ff = zr[:, R_FF:R_FF + LANES] + rows_ref[3:4, 0:LANES]
    lf = jnp.minimum(ff, 0.0) - jnp.log1p(jnp.exp(-jnp.abs(ff)))
    hi, mid, lo = _split3(lf)
    ltri = ltri_ref[...]
    cum = _dot(ltri, hi) + _dot(ltri, mid) + _dot(ltri, lo) + carry_ref[0:1, :]
    carry_ref[...] = jnp.broadcast_to(cum[ts - 1:ts, :], carry_ref.shape)
    chi, cmid, clo = _split3(cum * LOG2E)
    lane = lax.broadcasted_iota(jnp.int32, (ts, LANES), 1)
    cpieces = jnp.where(
        lane < HEADS, chi.astype(F32),
        jnp.where(lane < 2 * HEADS, pltpu.roll(cmid.astype(F32), HEADS, 1),
                  jnp.where(lane < 3 * HEADS, pltpu.roll(clo.astype(F32), 2 * HEADS, 1), 0.0)))
    cb = cpieces.astype(BF16)

    q_scale = HEAD_DIM ** -0.5 * LOG2E
    qf_ref[0] = (_dot(hb, wq_ref[...]) * q_scale + _dot(cb, eq_ref[...]) + rows_ref[0:1, :]).astype(BF16)
    kf_ref[0] = (_dot(hb, wk_ref[...]) + _dot(cb, ek_ref[...]) + rows_ref[1:2, :]).astype(BF16)
    vft_ref[0, 0] = (_dot_nt(wvt_ref[...], hb) + ones_rows).astype(BF16)
    sgf_ref[0] = _silu(zr[:, R_FG:R_FG + FOX_W]).astype(BF16)
    sgm_ref[0] = _silu(zr[:, R_MG:R_MG + FOX_W]).astype(BF16)

    cos = cos_ref[0]
    sin = sin_ref[0]
    cos8 = jnp.concatenate([cos] * HEADS, axis=1)
    sin8 = jnp.concatenate([sin] * HEADS, axis=1)
    qn = _rms(zr[:, R_QL:R_QL + Q_LORA], rows_ref[4:5, 0:Q_LORA]).astype(BF16)
    m_scale = MLA_QK ** -0.5 * LOG2E
    qm = (_dot(qn, wuq_ref[...]) * cos8 + _dot(qn, wuqr_ref[...]) * sin8) * m_scale
    qm_ref[0] = qm.astype(BF16)
    kvn = _rms(zr[:, R_KVL:R_KVL + KV_LORA], rows_ref[5:6, 0:KV_LORA]).astype(BF16)
    krope = zr[:, R_KR:R_KR + LANES] * cos + zr[:, R_KRR:R_KRR + LANES] * sin
    km_ref[0] = (_dot(kvn, wkk_ref[...]) + jnp.concatenate([krope] * HEADS, axis=1)).astype(BF16)
    vmt_ref[0, 0] = (_dot_nt(wkvt_ref[...], kvn) + ones_rows).astype(BF16)


def _resident(shape):
    nd = len(shape)
    return pl.BlockSpec(shape, lambda b, s: (0,) * nd, pipeline_mode=pl.Buffered(1))


def _project(x, mod_l, norm_g_l, lw, consts, cos_t, sin_t):
    batch, seq, _ = x.shape
    ts = SEQ_TILE
    eq, ek, ltri = consts
    row_tile = lambda w: pl.BlockSpec((1, ts, w), lambda b, s: (b, s, 0))
    wide = jax.ShapeDtypeStruct((batch, seq, HEAD_BLOCKS), BF16)
    narrow = jax.ShapeDtypeStruct((batch, seq, FOX_W), BF16)
    wide_t = jax.ShapeDtypeStruct((batch, seq // ts, HEAD_BLOCKS, ts), BF16)
    vt_tile = pl.BlockSpec((1, 1, HEAD_BLOCKS, ts), lambda b, s: (b, s, 0, 0))
    weights = [lw["wq"], lw["wk"], lw["wvt"], lw["wr"], lw["wuq"], lw["wuqr"], lw["wkk"], lw["wkvt"],
               eq, ek, lw["rows"], ltri]
    return pl.pallas_call(
        _proj_kernel,
        grid=(batch, seq // ts),
        in_specs=[row_tile(D_MODEL),
                  pl.BlockSpec((1, 1, 3 * D_MODEL), lambda b, s: (b, 0, 0)),
                  _resident((1, D_MODEL))]
                 + [_resident(w.shape) for w in weights]
                 + [row_tile(LANES), row_tile(LANES)],
        out_specs=[row_tile(HEAD_BLOCKS)] * 2 + [vt_tile, row_tile(FOX_W)]
                  + [row_tile(HEAD_BLOCKS)] * 2 + [vt_tile, row_tile(FOX_W)],
        out_shape=[wide, wide, wide_t, narrow, wide, wide, wide_t, narrow],
        scratch_shapes=[pltpu.VMEM((8, LANES), F32)],
        compiler_params=pltpu.CompilerParams(
            dimension_semantics=("arbitrary", "arbitrary"), vmem_limit_bytes=VMEM_LIMIT),
        name="in_proj",
    )(x, mod_l.reshape(batch, 1, 3 * D_MODEL), norm_g_l.reshape(1, D_MODEL), *weights, cos_t, sin_t)


def _attn_kernel(q_ref, k_ref, vt_ref, sg_ref, o_ref, m_ref, acc_ref, *, chunk):
    tq = q_ref.shape[1]
    i = pl.program_id(1)
    m_ref[...] = jnp.full(m_ref.shape, NEG_BIG, F32)
    acc_ref[...] = jnp.zeros(acc_ref.shape, F32)

    key = lax.broadcasted_iota(jnp.int32, (tq, tq), 0)
    qry = lax.broadcasted_iota(jnp.int32, (tq, tq), 1)
    if chunk == 1:
        diag_mask = key <= qry
    else:
        diag_mask = (key // chunk) <= (qry // chunk)

    def scores(j, h):
        start = pl.multiple_of(j * tq, tq)
        blk = slice(h * LANES, (h + 1) * LANES)
        return _dot_nt(k_ref[0, pl.ds(start, tq), blk], q_ref[0, :, blk])

    def kv_step(j, st_first, masked):
        st_next = st_first
        for h in range(HEADS):
            blk = slice(h * LANES, (h + 1) * LANES)
            st = st_next
            if h + 1 < HEADS:
                st_next = scores(j, h + 1)
            elif not masked:
                st_next = scores(j + 1, 0)
            if masked:
                st = jnp.where(diag_mask, st, NEG_BIG)
            m_prev = m_ref[h]
            m_new = jnp.maximum(m_prev, jnp.max(st, axis=0, keepdims=True))
            alpha = jnp.exp2(m_prev - m_new)
            p = jnp.exp2(st - pltpu.repeat(m_new, tq // 8, 0))
            vt = vt_ref[0, j, blk, :]
            acc_ref[h] = acc_ref[h] * pltpu.repeat(alpha, LANES // 8, 0) + _dot(vt, p.astype(BF16))
            m_ref[h] = m_new
        return st_next

    st_diag = lax.fori_loop(0, i, lambda j, st: kv_step(j, st, False), scores(0, 0))
    kv_step(i, st_diag, True)

    for pair in range(HEADS // 2):
        halves = []
        for h in (2 * pair, 2 * pair + 1):
            a = acc_ref[h]
            halves.append(a[0:HEAD_DIM, :] / a[HEAD_DIM:HEAD_DIM + 1, :])
        out = jnp.concatenate(halves, axis=0).T
        sg = sg_ref[0, :, pair * LANES:(pair + 1) * LANES].astype(F32)
        o_ref[0, :, pair * LANES:(pair + 1) * LANES] = (out * sg).astype(o_ref.dtype)


def _attention(q, k, vt, sg, chunk):
    batch, seq, _ = q.shape
    tq = Q_TILE
    assert vt.shape == (batch, seq // tq, HEAD_BLOCKS, tq)
    return pl.pallas_call(
        functools.partial(_attn_kernel, chunk=chunk),
        grid=(batch, seq // tq),
        in_specs=[
            pl.BlockSpec((1, tq, HEAD_BLOCKS), lambda b, i: (b, i, 0)),
            pl.BlockSpec((1, seq, HEAD_BLOCKS), lambda b, i: (b, 0, 0)),
            pl.BlockSpec((1, seq // tq, HEAD_BLOCKS, tq), lambda b, i: (b, 0, 0, 0)),
            pl.BlockSpec((1, tq, FOX_W), lambda b, i: (b, i, 0)),
        ],
        out_specs=pl.BlockSpec((1, tq, FOX_W), lambda b, i: (b, i, 0)),
        out_shape=jax.ShapeDtypeStruct((batch, seq, FOX_W), BF16),
        scratch_shapes=[pltpu.VMEM((HEADS, 8, tq), F32), pltpu.VMEM((HEADS, LANES, tq), F32)],
        compiler_params=pltpu.CompilerParams(
            dimension_semantics=("arbitrary", "arbitrary"), vmem_limit_bytes=VMEM_LIMIT),
        name="attn_fox" if chunk == 1 else "attn_mla",
    )(q, k, vt, sg)


def _out_kernel(x_ref, yf_ref, ym_ref, w_ref, mod_ref, fg_ref, o_ref, *, final):
    y = _dot(yf_ref[0], w_ref[0:FOX_W, :]) + _dot(ym_ref[0], w_ref[FOX_W:2 * FOX_W, :])
    gate = mod_ref[0, :, 2 * D_MODEL:3 * D_MODEL]
    xn = x_ref[0] + gate * y
    if final:
        xn = _rms(xn, fg_ref[...])
    o_ref[0] = xn


def _out_project(x, yf, ym, wout, mod_l, final_g, final):
    batch, seq, _ = x.shape
    ts = SEQ_TILE
    row_tile = lambda w: pl.BlockSpec((1, ts, w), lambda b, s: (b, s, 0))
    return pl.pallas_call(
        functools.partial(_out_kernel, final=final),
        grid=(batch, seq // ts),
        in_specs=[row_tile(D_MODEL), row_tile(FOX_W), row_tile(FOX_W),
                  _resident((D_MODEL, D_MODEL)),
                  pl.BlockSpec((1, 1, 3 * D_MODEL), lambda b, s: (b, 0, 0)),
                  _resident((1, D_MODEL))],
        out_specs=row_tile(D_MODEL),
        out_shape=jax.ShapeDtypeStruct(x.shape, F32),
        compiler_params=pltpu.CompilerParams(
            dimension_semantics=("arbitrary", "arbitrary"), vmem_limit_bytes=VMEM_LIMIT),
        name="out_proj_final" if final else "out_proj",
    )(x, yf, ym, wout, mod_l.reshape(batch, 1, 3 * D_MODEL), final_g.reshape(1, D_MODEL))


def kernel(x, c, positions, norm_g, w_ada, b_ada, w_in, b_f, q_norm_g, w_uq, kv_norm_g, w_ukv,
           w_out, final_g):
    mod = _adaln(c, w_ada, b_ada)
    cos_t, sin_t = _rope_tables(positions)
    eq, ek = _aug_matrices()
    ltri = np.tril(np.ones((SEQ_TILE, SEQ_TILE), np.float32))
    consts = (jnp.asarray(eq, BF16), jnp.asarray(ek, BF16), jnp.asarray(ltri, BF16))
    for l in range(DEPTH):
        lw = _prep_layer(w_in[l], w_uq[l], w_ukv[l], w_out[l], b_f[l], q_norm_g[l], kv_norm_g[l])
        qf, kf, vf, sgf, qm, km, vm, sgm = _project(x, mod[l], norm_g[l], lw, consts, cos_t, sin_t)
        yf = _attention(qf, kf, vf, sgf, chunk=1)
        ym = _attention(qm, km, vm, sgm, chunk=CHUNK)
        x = _out_project(x, yf, ym, lw["wout"], mod[l], final_g, final=(l == DEPTH - 1))
    return x
```

```python
import functools
import math

import numpy as np
import jax
import jax.numpy as jnp
from jax import lax
from jax.experimental import pallas as pl
from jax.experimental.pallas import tpu as pltpu

D_MODEL = 1024
DEPTH = 4
CHUNK = 64
HEADS = 8
HEAD_DIM = 64
FOX_W = HEADS * HEAD_DIM
MLA_ROPE = 32
MLA_QK = HEAD_DIM + MLA_ROPE
Q_LORA = 256
KV_LORA = 128
ROPE_THETA = 10000.0
EPS = 1e-6

LANES = 128
HEAD_BLOCKS = HEADS * LANES
LOG2E = 1.4426950408889634
NEG_BIG = -1e30

SEQ_TILE = 512
Q_TILE = 512
SCORE_LOOKAHEAD = 3
V_ROWS = 80
VMEM_LIMIT = 56 * 1024 * 1024

AUG0 = HEAD_DIM

R_FF, R_FG, R_QL, R_KVL, R_KR, R_KRR, R_MG, R_END = 0, 128, 640, 896, 1024, 1152, 1280, 1792

F32 = jnp.float32
BF16 = jnp.bfloat16


def _dot(a, b):
    return jnp.dot(a, b, preferred_element_type=F32)


def _split3(v):
    hi = v.astype(BF16)
    r1 = v - hi.astype(F32)
    mid = r1.astype(BF16)
    lo = (r1 - mid.astype(F32)).astype(BF16)
    return hi, mid, lo


def _adaln_kernel(c_ref, w_ref, b_ref, o_ref):
    cv = c_ref[...]
    ca = (cv * jax.nn.sigmoid(cv)).astype(BF16)
    o_ref[0] = _dot(ca, w_ref[0].astype(BF16)) + b_ref[0]


def _adaln(c, w_ada, b_ada):
    batch = c.shape[0]
    n_tiles = 3
    return pl.pallas_call(
        _adaln_kernel,
        grid=(DEPTH, n_tiles),
        in_specs=[
            pl.BlockSpec((batch, D_MODEL), lambda l, n: (0, 0)),
            pl.BlockSpec((1, D_MODEL, D_MODEL), lambda l, n: (l, 0, n)),
            pl.BlockSpec((1, 1, D_MODEL), lambda l, n: (l, 0, n)),
        ],
        out_specs=pl.BlockSpec((1, batch, D_MODEL), lambda l, n: (l, 0, n)),
        out_shape=jax.ShapeDtypeStruct((DEPTH, batch, 3 * D_MODEL), F32),
        compiler_params=pltpu.CompilerParams(
            dimension_semantics=("arbitrary", "arbitrary"), vmem_limit_bytes=VMEM_LIMIT),
        name="adaln_mod",
    )(c, w_ada, b_ada.reshape(DEPTH, 1, 3 * D_MODEL))


def _rope_table_kernel(pos_ref, invf_ref, cos_ref, sin_ref):
    ang = pos_ref[0].astype(F32) * invf_ref[...]
    cos_ref[0] = jnp.cos(ang)
    sin_ref[0] = jnp.sin(ang)


def _rope_tables(positions):
    batch, seq = positions.shape
    inv_freq = 1.0 / (ROPE_THETA ** (jnp.arange(0, MLA_ROPE, 2, dtype=F32) / MLA_ROPE))
    half = MLA_ROPE // 2
    invf_row = jnp.zeros((1, LANES), F32)
    invf_row = invf_row.at[0, HEAD_DIM:HEAD_DIM + half].set(inv_freq)
    invf_row = invf_row.at[0, HEAD_DIM + half:HEAD_DIM + MLA_ROPE].set(inv_freq)
    ts = SEQ_TILE
    out = jax.ShapeDtypeStruct((batch, seq, LANES), F32)
    return pl.pallas_call(
        _rope_table_kernel,
        grid=(batch, seq // ts),
        in_specs=[
            pl.BlockSpec((1, ts, 1), lambda b, s: (b, s, 0)),
            pl.BlockSpec((1, LANES), lambda b, s: (0, 0)),
        ],
        out_specs=[pl.BlockSpec((1, ts, LANES), lambda b, s: (b, s, 0))] * 2,
        out_shape=[out, out],
        compiler_params=pltpu.CompilerParams(
            dimension_semantics=("arbitrary", "arbitrary"), vmem_limit_bytes=VMEM_LIMIT),
        name="rope_tables",
    )(positions.reshape(batch, seq, 1), invf_row)


def _head_blocks(w, width):
    k = w.shape[0]
    w3 = w.reshape(k, HEADS, width)
    return jnp.pad(w3, ((0, 0), (0, 0), (0, LANES - width))).reshape(k, HEADS * LANES)


def _rot_half_cols(w):
    half = MLA_ROPE // 2
    return jnp.concatenate([-w[..., half:], w[..., :half]], axis=-1)


def _const_rows():
    rows = np.zeros((8, HEAD_BLOCKS), np.float32)
    for h in range(HEADS):
        base = h * LANES
        rows[0, base + AUG0 + 3:base + AUG0 + 6] = 1.0
        rows[1, base + AUG0:base + AUG0 + 3] = 1.0
    return rows


def _aug_matrices():
    eq = np.zeros((LANES, HEAD_BLOCKS), np.float32)
    ek = np.zeros((LANES, HEAD_BLOCKS), np.float32)
    for p in range(3):
        for h in range(HEADS):
            eq[p * HEADS + h, h * LANES + AUG0 + p] = 1.0
            ek[p * HEADS + h, h * LANES + AUG0 + 3 + p] = -1.0
    return eq, ek


def _prep_layer(w_in_l, w_uq_l, w_ukv_l, w_out_l, b_f_l, q_norm_g_l, kv_norm_g_l):
    o = np.cumsum([0, FOX_W, FOX_W, FOX_W, HEADS, FOX_W, Q_LORA, KV_LORA, MLA_ROPE, FOX_W])
    w_fq, w_fk, w_fv, w_ff, w_fg, w_ql, w_kvl, w_kr, w_mg = (
        w_in_l[:, o[i]:o[i + 1]] for i in range(9))
    wq = _head_blocks(w_fq, HEAD_DIM).astype(BF16)
    wk = _head_blocks(w_fk, HEAD_DIM).astype(BF16)
    wvt = _head_blocks(w_fv, HEAD_DIM).T.astype(BF16)
    pad_ff = jnp.pad(w_ff, ((0, 0), (0, LANES - HEADS)))
    kr_blk = jnp.pad(w_kr, ((0, 0), (HEAD_DIM, LANES - MLA_QK)))
    krr_blk = jnp.pad(_rot_half_cols(w_kr), ((0, 0), (HEAD_DIM, LANES - MLA_QK)))
    wr = jnp.concatenate([pad_ff, w_fg, w_ql, w_kvl, kr_blk, krr_blk, w_mg], axis=1).astype(BF16)

    uq3 = w_uq_l.reshape(Q_LORA, HEADS, MLA_QK)
    wuq = jnp.pad(uq3, ((0, 0), (0, 0), (0, LANES - MLA_QK))).reshape(Q_LORA, HEAD_BLOCKS).astype(BF16)
    uq_rot = jnp.pad(_rot_half_cols(uq3[..., HEAD_DIM:]), ((0, 0), (0, 0), (HEAD_DIM, LANES - MLA_QK)))
    wuqr = uq_rot.reshape(Q_LORA, HEAD_BLOCKS).astype(BF16)
    ukv3 = w_ukv_l.reshape(KV_LORA, HEADS, 2 * HEAD_DIM)
    wkk = _head_blocks(ukv3[..., :HEAD_DIM].reshape(KV_LORA, FOX_W), HEAD_DIM).astype(BF16)
    wkvt = _head_blocks(ukv3[..., HEAD_DIM:].reshape(KV_LORA, FOX_W), HEAD_DIM).T.astype(BF16)

    rows = jnp.asarray(_const_rows())
    rows = rows.at[3, :HEADS].set(b_f_l)
    rows = rows.at[4, :Q_LORA].set(q_norm_g_l)
    rows = rows.at[5, :KV_LORA].set(kv_norm_g_l)
    return dict(wq=wq, wk=wk, wvt=wvt, wr=wr, wuq=wuq, wuqr=wuqr, wkk=wkk, wkvt=wkvt, rows=rows,
                wout=w_out_l.astype(BF16))


def _rms(v, g):
    return v * lax.rsqrt(jnp.mean(v * v, axis=-1, keepdims=True) + EPS) * g


def _silu(v):
    return v * jax.nn.sigmoid(v)


def _dot_nt(a, b):
    return lax.dot_general(a, b, (((1,), (1,)), ((), ())), preferred_element_type=F32)


def _proj_kernel(x_ref, mod_ref, g_ref, wq_ref, wk_ref, wvt_ref, wr_ref, wuq_ref, wuqr_ref,
                 wkk_ref, wkvt_ref, eq_ref, ek_ref, rows_ref, ltri_ref, cos_ref, sin_ref,
                 qf_ref, kf_ref, vft_ref, sgf_ref, qm_ref, km_ref, vmt_ref, sgm_ref, carry_ref):
    ts = x_ref.shape[1]
    vrow = lax.broadcasted_iota(jnp.int32, (HEAD_BLOCKS, ts), 0)
    ones_rows = jnp.where((vrow & (LANES - 1)) == HEAD_DIM, 1.0, 0.0)

    @pl.when(pl.program_id(1) == 0)
    def _():
        carry_ref[...] = jnp.zeros_like(carry_ref)

    xs = x_ref[0]
    shift = mod_ref[0, :, 0:D_MODEL]
    scale = mod_ref[0, :, D_MODEL:2 * D_MODEL]
    h = _rms(xs, g_ref[...]) * (1.0 + scale) + shift
    hb = h.astype(BF16)

    zr = _dot(hb, wr_ref[...])

    ff = zr[:, R_FF:R_FF + LANES] + rows_ref[3:4, 0:LANES]
    lf = jnp.minimum(ff, 0.0) - jnp.log1p(jnp.exp(-jnp.abs(ff)))
    hi, mid, lo = _split3(lf)
    ltri = ltri_ref[...]
    cum = _dot(ltri, hi) + _dot(ltri, mid) + _dot(ltri, lo) + carry_ref[0:1, :]
    carry_ref[...] = jnp.broadcast_to(cum[ts - 1:ts, :], carry_ref.shape)
    chi, cmid, clo = _split3(cum * LOG2E)
    lane = lax.broadcasted_iota(jnp.int32, (ts, LANES), 1)
    cpieces = jnp.where(
        lane < HEADS, chi.astype(F32),
        jnp.where(lane < 2 * HEADS, pltpu.roll(cmid.astype(F32), HEADS, 1),
                  jnp.where(lane < 3 * HEADS, pltpu.roll(clo.astype(F32), 2 * HEADS, 1), 0.0)))
    cb = cpieces.astype(BF16)

    q_scale = HEAD_DIM ** -0.5 * LOG2E
    qf_ref[0] = (_dot(hb, wq_ref[...]) * q_scale + _dot(cb, eq_ref[...]) + rows_ref[0:1, :]).astype(BF16)
    kf_ref[0] = (_dot(hb, wk_ref[...]) + _dot(cb, ek_ref[...]) + rows_ref[1:2, :]).astype(BF16)
    vft_ref[0, 0] = (_dot_nt(wvt_ref[...], hb) + ones_rows).astype(BF16)
    sgf_ref[0] = _silu(zr[:, R_FG:R_FG + FOX_W]).astype(BF16)
    sgm_ref[0] = _silu(zr[:, R_MG:R_MG + FOX_W]).astype(BF16)

    cos = cos_ref[0]
    sin = sin_ref[0]
    cos8 = jnp.concatenate([cos] * HEADS, axis=1)
    sin8 = jnp.concatenate([sin] * HEADS, axis=1)
    qn = _rms(zr[:, R_QL:R_QL + Q_LORA], rows_ref[4:5, 0:Q_LORA]).astype(BF16)
    m_scale = MLA_QK ** -0.5 * LOG2E
    qm = (_dot(qn, wuq_ref[...]) * cos8 + _dot(qn, wuqr_ref[...]) * sin8) * m_scale
    qm_ref[0] = qm.astype(BF16)
    kvn = _rms(zr[:, R_KVL:R_KVL + KV_LORA], rows_ref[5:6, 0:KV_LORA]).astype(BF16)
    krope = zr[:, R_KR:R_KR + LANES] * cos + zr[:, R_KRR:R_KRR + LANES] * sin
    km_ref[0] = (_dot(kvn, wkk_ref[...]) + jnp.concatenate([krope] * HEADS, axis=1)).astype(BF16)
    vmt_ref[0, 0] = (_dot_nt(wkvt_ref[...], kvn) + ones_rows).astype(BF16)


def _resident(shape):
    nd = len(shape)
    return pl.BlockSpec(shape, lambda b, s: (0,) * nd, pipeline_mode=pl.Buffered(1))


def _project(x, mod_l, norm_g_l, lw, consts, cos_t, sin_t):
    batch, seq, _ = x.shape
    ts = SEQ_TILE
    eq, ek, ltri = consts
    row_tile = lambda w: pl.BlockSpec((1, ts, w), lambda b, s: (b, s, 0))
    wide = jax.ShapeDtypeStruct((batch, seq, HEAD_BLOCKS), BF16)
    narrow = jax.ShapeDtypeStruct((batch, seq, FOX_W), BF16)
    wide_t = jax.ShapeDtypeStruct((batch, seq // ts, HEAD_BLOCKS, ts), BF16)
    vt_tile = pl.BlockSpec((1, 1, HEAD_BLOCKS, ts), lambda b, s: (b, s, 0, 0))
    weights = [lw["wq"], lw["wk"], lw["wvt"], lw["wr"], lw["wuq"], lw["wuqr"], lw["wkk"], lw["wkvt"],
               eq, ek, lw["rows"], ltri]
    return pl.pallas_call(
        _proj_kernel,
        grid=(batch, seq // ts),
        in_specs=[row_tile(D_MODEL),
                  pl.BlockSpec((1, 1, 3 * D_MODEL), lambda b, s: (b, 0, 0)),
                  _resident((1, D_MODEL))]
                 + [_resident(w.shape) for w in weights]
                 + [row_tile(LANES), row_tile(LANES)],
        out_specs=[row_tile(HEAD_BLOCKS)] * 2 + [vt_tile, row_tile(FOX_W)]
                  + [row_tile(HEAD_BLOCKS)] * 2 + [vt_tile, row_tile(FOX_W)],
        out_shape=[wide, wide, wide_t, narrow, wide, wide, wide_t, narrow],
        scratch_shapes=[pltpu.VMEM((8, LANES), F32)],
        compiler_params=pltpu.CompilerParams(
            dimension_semantics=("arbitrary", "arbitrary"), vmem_limit_bytes=VMEM_LIMIT),
        name="in_proj",
    )(x, mod_l.reshape(batch, 1, 3 * D_MODEL), norm_g_l.reshape(1, D_MODEL), *weights, cos_t, sin_t)


def _attn_kernel(q_ref, k_ref, vt_ref, sg_ref, o_ref, m_ref, acc_ref, ahead_ref, *, chunk):
    tq = q_ref.shape[1]
    i = pl.program_id(1)
    m_ref[...] = jnp.full(m_ref.shape, NEG_BIG, F32)
    acc_ref[...] = jnp.zeros(acc_ref.shape, F32)

    half = tq // 2
    units = [(h, part) for h in range(HEADS) for part in range(2)]

    def diag_mask(nkeys, part):
        key = lax.broadcasted_iota(jnp.int32, (nkeys, half), 0)
        qry = lax.broadcasted_iota(jnp.int32, (nkeys, half), 1) + part * half
        if chunk == 1:
            return key <= qry
        return (key // chunk) <= (qry // chunk)

    def scores(j, u, nkeys):
        h, part = units[u]
        start = pl.multiple_of(j * tq, tq)
        blk = slice(h * LANES, (h + 1) * LANES)
        return _dot_nt(k_ref[0, pl.ds(start, nkeys), blk],
                       q_ref[0, part * half:(part + 1) * half, blk])

    ring = ahead_ref.shape[0]
    assert len(units) % ring == 0 and ring > SCORE_LOOKAHEAD

    def kv_step(j, masked):
        def unit_keys(u):
            return half if (masked and units[u][1] == 0 and u >= SCORE_LOOKAHEAD) else tq

        for u, (h, part) in enumerate(units):
            cols = slice(part * half, (part + 1) * half)
            nkeys = unit_keys(u)
            ahead = u + SCORE_LOOKAHEAD
            if ahead < len(units):
                ahead_ref[ahead % ring, 0:unit_keys(ahead), :] = scores(j, ahead, unit_keys(ahead))
            elif not masked:
                ahead_ref[ahead % ring] = scores(j + 1, ahead - len(units), tq)

            def load_scores():
                st = ahead_ref[u % ring, 0:nkeys, :]
                return jnp.where(diag_mask(nkeys, part), st, NEG_BIG) if masked else st

            m_prev = m_ref[h, :, cols]
            m_new = jnp.maximum(m_prev, jnp.max(load_scores(), axis=0, keepdims=True))
            alpha = jnp.exp2(m_prev - m_new)
            p = jnp.exp2(load_scores() - jnp.tile(m_new, (nkeys // 8, 1)))
            vt = vt_ref[0, j, h * LANES:h * LANES + V_ROWS, 0:nkeys]
            acc_ref[h, :, cols] = (acc_ref[h, :, cols] * jnp.tile(alpha, (V_ROWS // 8, 1))
                                   + _dot(vt, p.astype(BF16)))
            m_ref[h, :, cols] = m_new

    for s in range(SCORE_LOOKAHEAD):
        ahead_ref[s] = scores(0, s, tq)

    @pl.loop(0, i)
    def _(j):
        kv_step(j, False)

    kv_step(i, True)

    for pair in range(HEADS // 2):
        halves = []
        for h in (2 * pair, 2 * pair + 1):
            a = acc_ref[h]
            halves.append(a[0:HEAD_DIM, :] / a[HEAD_DIM:HEAD_DIM + 1, :])
        out = jnp.concatenate(halves, axis=0).T
        sg = sg_ref[0, :, pair * LANES:(pair + 1) * LANES].astype(F32)
        o_ref[0, :, pair * LANES:(pair + 1) * LANES] = (out * sg).astype(o_ref.dtype)


def _attention(q, k, vt, sg, chunk):
    batch, seq, _ = q.shape
    tq = Q_TILE
    assert vt.shape == (batch, seq // tq, HEAD_BLOCKS, tq)
    return pl.pallas_call(
        functools.partial(_attn_kernel, chunk=chunk),
        grid=(batch, seq // tq),
        in_specs=[
            pl.BlockSpec((1, tq, HEAD_BLOCKS), lambda b, i: (b, i, 0)),
            pl.BlockSpec((1, seq, HEAD_BLOCKS), lambda b, i: (b, 0, 0)),
            pl.BlockSpec((1, seq // tq, HEAD_BLOCKS, tq), lambda b, i: (b, 0, 0, 0)),
            pl.BlockSpec((1, tq, FOX_W), lambda b, i: (b, i, 0)),
        ],
        out_specs=pl.BlockSpec((1, tq, FOX_W), lambda b, i: (b, i, 0)),
        out_shape=jax.ShapeDtypeStruct((batch, seq, FOX_W), BF16),
        scratch_shapes=[pltpu.VMEM((HEADS, 8, tq), F32), pltpu.VMEM((HEADS, V_ROWS, tq), F32),
                        pltpu.VMEM((SCORE_LOOKAHEAD + 1, tq, tq // 2), F32)],
        compiler_params=pltpu.CompilerParams(
            dimension_semantics=("arbitrary", "arbitrary"), vmem_limit_bytes=VMEM_LIMIT),
        name="attn_fox" if chunk == 1 else "attn_mla",
    )(q, k, vt, sg)


def _out_kernel(x_ref, yf_ref, ym_ref, w_ref, mod_ref, fg_ref, o_ref, *, final):
    y = _dot(yf_ref[0], w_ref[0:FOX_W, :]) + _dot(ym_ref[0], w_ref[FOX_W:2 * FOX_W, :])
    gate = mod_ref[0, :, 2 * D_MODEL:3 * D_MODEL]
    xn = x_ref[0] + gate * y
    if final:
        xn = _rms(xn, fg_ref[...])
    o_ref[0] = xn


def _out_project(x, yf, ym, wout, mod_l, final_g, final):
    batch, seq, _ = x.shape
    ts = SEQ_TILE
    row_tile = lambda w: pl.BlockSpec((1, ts, w), lambda b, s: (b, s, 0))
    return pl.pallas_call(
        functools.partial(_out_kernel, final=final),
        grid=(batch, seq // ts),
        in_specs=[row_tile(D_MODEL), row_tile(FOX_W), row_tile(FOX_W),
                  _resident((D_MODEL, D_MODEL)),
                  pl.BlockSpec((1, 1, 3 * D_MODEL), lambda b, s: (b, 0, 0)),
                  _resident((1, D_MODEL))],
        out_specs=row_tile(D_MODEL),
        out_shape=jax.ShapeDtypeStruct(x.shape, F32),
        compiler_params=pltpu.CompilerParams(
            dimension_semantics=("arbitrary", "arbitrary"), vmem_limit_bytes=VMEM_LIMIT),
        name="out_proj_final" if final else "out_proj",
    )(x, yf, ym, wout, mod_l.reshape(batch, 1, 3 * D_MODEL), final_g.reshape(1, D_MODEL))


def kernel(x, c, positions, norm_g, w_ada, b_ada, w_in, b_f, q_norm_g, w_uq, kv_norm_g, w_ukv,
           w_out, final_g):
    mod = _adaln(c, w_ada, b_ada)
    cos_t, sin_t = _rope_tables(positions)
    eq, ek = _aug_matrices()
    ltri = np.tril(np.ones((SEQ_TILE, SEQ_TILE), np.float32))
    consts = (jnp.asarray(eq, BF16), jnp.asarray(ek, BF16), jnp.asarray(ltri, BF16))
    for l in range(DEPTH):
        lw = _prep_layer(w_in[l], w_uq[l], w_ukv[l], w_out[l], b_f[l], q_norm_g[l], kv_norm_g[l])
        qf, kf, vf, sgf, qm, km, vm, sgm = _project(x, mod[l], norm_g[l], lw, consts, cos_t, sin_t)
        yf = _attention(qf, kf, vf, sgf, chunk=1)
        ym = _attention(qm, km, vm, sgm, chunk=CHUNK)
        x = _out_project(x, yf, ym, lw["wout"], mod[l], final_g, final=(l == DEPTH - 1))
    return x
```

```python
import functools
import math

import numpy as np
import jax
import jax.numpy as jnp
from jax import lax
from jax.experimental import pallas as pl
from jax.experimental.pallas import tpu as pltpu

D_MODEL = 1024
DEPTH = 4
CHUNK = 64
HEADS = 8
HEAD_DIM = 64
FOX_W = HEADS * HEAD_DIM
MLA_ROPE = 32
MLA_QK = HEAD_DIM + MLA_ROPE
Q_LORA = 256
KV_LORA = 128
ROPE_THETA = 10000.0
EPS = 1e-6

LANES = 128
HEAD_BLOCKS = HEADS * LANES
LOG2E = 1.4426950408889634
NEG_BIG = -1e30

SEQ_TILE = 512
Q_TILE = 512
SCORE_LOOKAHEAD = 3
V_ROWS = 80
VMEM_LIMIT = 56 * 1024 * 1024


R_FF, R_FG, R_QL, R_KVL, R_KR, R_KRR, R_MG, R_END = 0, 128, 640, 896, 1024, 1152, 1280, 1792

F32 = jnp.float32
BF16 = jnp.bfloat16


def _dot(a, b):
    return jnp.dot(a, b, preferred_element_type=F32)


def _split3(v):
    hi = v.astype(BF16)
    r1 = v - hi.astype(F32)
    mid = r1.astype(BF16)
    lo = (r1 - mid.astype(F32)).astype(BF16)
    return hi, mid, lo


def _adaln_kernel(c_ref, w_ref, b_ref, o_ref):
    cv = c_ref[...]
    ca = (cv * jax.nn.sigmoid(cv)).astype(BF16)
    o_ref[0] = _dot(ca, w_ref[0].astype(BF16)) + b_ref[0]


def _adaln(c, w_ada, b_ada):
    batch = c.shape[0]
    n_tiles = 3
    return pl.pallas_call(
        _adaln_kernel,
        grid=(DEPTH, n_tiles),
        in_specs=[
            pl.BlockSpec((batch, D_MODEL), lambda l, n: (0, 0)),
            pl.BlockSpec((1, D_MODEL, D_MODEL), lambda l, n: (l, 0, n)),
            pl.BlockSpec((1, 1, D_MODEL), lambda l, n: (l, 0, n)),
        ],
        out_specs=pl.BlockSpec((1, batch, D_MODEL), lambda l, n: (l, 0, n)),
        out_shape=jax.ShapeDtypeStruct((DEPTH, batch, 3 * D_MODEL), F32),
        compiler_params=pltpu.CompilerParams(
            dimension_semantics=("arbitrary", "arbitrary"), vmem_limit_bytes=VMEM_LIMIT),
        name="adaln_mod",
    )(c, w_ada, b_ada.reshape(DEPTH, 1, 3 * D_MODEL))


def _rope_table_kernel(pos_ref, invf_ref, cos_ref, sin_ref):
    ang = pos_ref[0].astype(F32) * invf_ref[...]
    cos_ref[0] = jnp.cos(ang)
    sin_ref[0] = jnp.sin(ang)


def _rope_tables(positions):
    batch, seq = positions.shape
    inv_freq = 1.0 / (ROPE_THETA ** (jnp.arange(0, MLA_ROPE, 2, dtype=F32) / MLA_ROPE))
    half = MLA_ROPE // 2
    invf_row = jnp.zeros((1, LANES), F32)
    invf_row = invf_row.at[0, HEAD_DIM:HEAD_DIM + half].set(inv_freq)
    invf_row = invf_row.at[0, HEAD_DIM + half:HEAD_DIM + MLA_ROPE].set(inv_freq)
    ts = SEQ_TILE
    out = jax.ShapeDtypeStruct((batch, seq, LANES), F32)
    return pl.pallas_call(
        _rope_table_kernel,
        grid=(batch, seq // ts),
        in_specs=[
            pl.BlockSpec((1, ts, 1), lambda b, s: (b, s, 0)),
            pl.BlockSpec((1, LANES), lambda b, s: (0, 0)),
        ],
        out_specs=[pl.BlockSpec((1, ts, LANES), lambda b, s: (b, s, 0))] * 2,
        out_shape=[out, out],
        compiler_params=pltpu.CompilerParams(
            dimension_semantics=("arbitrary", "arbitrary"), vmem_limit_bytes=VMEM_LIMIT),
        name="rope_tables",
    )(positions.reshape(batch, seq, 1), invf_row)


def _head_blocks(w, width):
    k = w.shape[0]
    w3 = w.reshape(k, HEADS, width)
    return jnp.pad(w3, ((0, 0), (0, 0), (0, LANES - width))).reshape(k, HEADS * LANES)


def _rot_half_cols(w):
    half = MLA_ROPE // 2
    return jnp.concatenate([-w[..., half:], w[..., :half]], axis=-1)


def _fox_aug_lane(h):
    return h * LANES + (HEAD_DIM if h % 2 == 0 else 0)


def _const_rows():
    rows = np.zeros((8, HEAD_BLOCKS), np.float32)
    for h in range(HEADS):
        base = _fox_aug_lane(h)
        rows[0, base + 3:base + 6] = 1.0
        rows[1, base:base + 3] = 1.0
    return rows


def _aug_matrices():
    eq = np.zeros((LANES, HEAD_BLOCKS), np.float32)
    ek = np.zeros((LANES, HEAD_BLOCKS), np.float32)
    for p in range(3):
        for h in range(HEADS):
            eq[p * HEADS + h, _fox_aug_lane(h) + p] = 1.0
            ek[p * HEADS + h, _fox_aug_lane(h) + 3 + p] = -1.0
    return eq, ek


def _prep_layer(w_in_l, w_uq_l, w_ukv_l, w_out_l, b_f_l, q_norm_g_l, kv_norm_g_l):
    o = np.cumsum([0, FOX_W, FOX_W, FOX_W, HEADS, FOX_W, Q_LORA, KV_LORA, MLA_ROPE, FOX_W])
    w_fq, w_fk, w_fv, w_ff, w_fg, w_ql, w_kvl, w_kr, w_mg = (
        w_in_l[:, o[i]:o[i + 1]] for i in range(9))
    wq = w_fq.astype(BF16)
    wk = w_fk.astype(BF16)
    wvt = w_fv.T.astype(BF16)
    pad_ff = jnp.pad(w_ff, ((0, 0), (0, LANES - HEADS)))
    kr_blk = jnp.pad(w_kr, ((0, 0), (HEAD_DIM, LANES - MLA_QK)))
    krr_blk = jnp.pad(_rot_half_cols(w_kr), ((0, 0), (HEAD_DIM, LANES - MLA_QK)))
    wr = jnp.concatenate([pad_ff, w_fg, w_ql, w_kvl, kr_blk, krr_blk, w_mg], axis=1).astype(BF16)

    uq3 = w_uq_l.reshape(Q_LORA, HEADS, MLA_QK)
    wuq = jnp.pad(uq3, ((0, 0), (0, 0), (0, LANES - MLA_QK))).reshape(Q_LORA, HEAD_BLOCKS).astype(BF16)
    uq_rot = jnp.pad(_rot_half_cols(uq3[..., HEAD_DIM:]), ((0, 0), (0, 0), (HEAD_DIM, LANES - MLA_QK)))
    wuqr = uq_rot.reshape(Q_LORA, HEAD_BLOCKS).astype(BF16)
    ukv3 = w_ukv_l.reshape(KV_LORA, HEADS, 2 * HEAD_DIM)
    wkk = _head_blocks(ukv3[..., :HEAD_DIM].reshape(KV_LORA, FOX_W), HEAD_DIM).astype(BF16)
    wkvt = ukv3[..., HEAD_DIM:].reshape(KV_LORA, FOX_W).T.astype(BF16)

    rows = jnp.asarray(_const_rows())
    rows = rows.at[3, :HEADS].set(b_f_l)
    rows = rows.at[4, :Q_LORA].set(q_norm_g_l)
    rows = rows.at[5, :KV_LORA].set(kv_norm_g_l)
    return dict(wq=wq, wk=wk, wvt=wvt, wr=wr, wuq=wuq, wuqr=wuqr, wkk=wkk, wkvt=wkvt, rows=rows,
                wout=w_out_l.astype(BF16))


def _rms(v, g):
    return v * lax.rsqrt(jnp.mean(v * v, axis=-1, keepdims=True) + EPS) * g


def _silu(v):
    return v * jax.nn.sigmoid(v)


def _dot_nt(a, b):
    return lax.dot_general(a, b, (((1,), (1,)), ((), ())), preferred_element_type=F32)


def _proj_kernel(x_ref, mod_ref, g_ref, wq_ref, wk_ref, wvt_ref, wr_ref, wuq_ref, wuqr_ref,
                 wkk_ref, wkvt_ref, eq_ref, ek_ref, rows_ref, ltri_ref, cos_ref, sin_ref,
                 qf_ref, kf_ref, vft_ref, sgf_ref, qm_ref, km_ref, vmt_ref, sgm_ref, carry_ref):
    ts = x_ref.shape[1]
    pad_rows = lax.broadcasted_iota(jnp.int32, (V_ROWS - HEAD_DIM, ts), 0)
    ones_then_zeros = jnp.where(pad_rows == 0, 1.0, 0.0).astype(BF16)

    def store_value_blocks(out_ref, vt):
        for h in range(HEADS):
            out_ref[0, 0, h * V_ROWS:h * V_ROWS + HEAD_DIM, :] = (
                vt[h * HEAD_DIM:(h + 1) * HEAD_DIM, :].astype(BF16))
            out_ref[0, 0, h * V_ROWS + HEAD_DIM:(h + 1) * V_ROWS, :] = ones_then_zeros

    @pl.when(pl.program_id(1) == 0)
    def _():
        carry_ref[...] = jnp.zeros_like(carry_ref)

    xs = x_ref[0]
    shift = mod_ref[0, :, 0:D_MODEL]
    scale = mod_ref[0, :, D_MODEL:2 * D_MODEL]
    h = _rms(xs, g_ref[...]) * (1.0 + scale) + shift
    hb = h.astype(BF16)

    zr = _dot(hb, wr_ref[...])

    ff = zr[:, R_FF:R_FF + LANES] + rows_ref[3:4, 0:LANES]
    lf = jnp.minimum(ff, 0.0) - jnp.log1p(jnp.exp(-jnp.abs(ff)))
    hi, mid, lo = _split3(lf)
    ltri = ltri_ref[...]
    cum = _dot(ltri, hi) + _dot(ltri, mid) + _dot(ltri, lo) + carry_ref[0:1, :]
    carry_ref[...] = jnp.broadcast_to(cum[ts - 1:ts, :], carry_ref.shape)
    chi, cmid, clo = _split3(cum * LOG2E)
    lane = lax.broadcasted_iota(jnp.int32, (ts, LANES), 1)
    cpieces = jnp.where(
        lane < HEADS, chi.astype(F32),
        jnp.where(lane < 2 * HEADS, pltpu.roll(cmid.astype(F32), HEADS, 1),
                  jnp.where(lane < 3 * HEADS, pltpu.roll(clo.astype(F32), 2 * HEADS, 1), 0.0)))
    cb = cpieces.astype(BF16)

    q_scale = HEAD_DIM ** -0.5 * LOG2E
    zq = _dot(hb, wq_ref[...]) * q_scale
    zk = _dot(hb, wk_ref[...])
    aug_q = _dot(cb, eq_ref[...]) + rows_ref[0:1, :]
    aug_k = _dot(cb, ek_ref[...]) + rows_ref[1:2, :]
    for h in range(HEADS):
        pair = slice((h // 2) * LANES, (h // 2 + 1) * LANES)
        blk = slice(h * LANES, (h + 1) * LANES)
        own = (lane < HEAD_DIM) if h % 2 == 0 else (lane >= HEAD_DIM)
        qf_ref[0, :, blk] = jnp.where(own, zq[:, pair], aug_q[:, blk]).astype(BF16)
        kf_ref[0, :, blk] = jnp.where(own, zk[:, pair], aug_k[:, blk]).astype(BF16)
    store_value_blocks(vft_ref, _dot_nt(wvt_ref[...], hb))
    sgf_ref[0] = _silu(zr[:, R_FG:R_FG + FOX_W]).astype(BF16)
    sgm_ref[0] = _silu(zr[:, R_MG:R_MG + FOX_W]).astype(BF16)

    cos = cos_ref[0]
    sin = sin_ref[0]
    cos8 = jnp.concatenate([cos] * HEADS, axis=1)
    sin8 = jnp.concatenate([sin] * HEADS, axis=1)
    qn = _rms(zr[:, R_QL:R_QL + Q_LORA], rows_ref[4:5, 0:Q_LORA]).astype(BF16)
    m_scale = MLA_QK ** -0.5 * LOG2E
    qm = (_dot(qn, wuq_ref[...]) * cos8 + _dot(qn, wuqr_ref[...]) * sin8) * m_scale
    qm_ref[0] = qm.astype(BF16)
    kvn = _rms(zr[:, R_KVL:R_KVL + KV_LORA], rows_ref[5:6, 0:KV_LORA]).astype(BF16)
    krope = zr[:, R_KR:R_KR + LANES] * cos + zr[:, R_KRR:R_KRR + LANES] * sin
    km_ref[0] = (_dot(kvn, wkk_ref[...]) + jnp.concatenate([krope] * HEADS, axis=1)).astype(BF16)
    store_value_blocks(vmt_ref, _dot_nt(wkvt_ref[...], kvn))


def _resident(shape):
    nd = len(shape)
    return pl.BlockSpec(shape, lambda b, s: (0,) * nd, pipeline_mode=pl.Buffered(1))


def _project(x, mod_l, norm_g_l, lw, consts, cos_t, sin_t):
    batch, seq, _ = x.shape
    ts = SEQ_TILE
    eq, ek, ltri = consts
    row_tile = lambda w: pl.BlockSpec((1, ts, w), lambda b, s: (b, s, 0))
    wide = jax.ShapeDtypeStruct((batch, seq, HEAD_BLOCKS), BF16)
    narrow = jax.ShapeDtypeStruct((batch, seq, FOX_W), BF16)
    wide_t = jax.ShapeDtypeStruct((batch, seq // ts, HEADS * V_ROWS, ts), BF16)
    vt_tile = pl.BlockSpec((1, 1, HEADS * V_ROWS, ts), lambda b, s: (b, s, 0, 0))
    weights = [lw["wq"], lw["wk"], lw["wvt"], lw["wr"], lw["wuq"], lw["wuqr"], lw["wkk"], lw["wkvt"],
               eq, ek, lw["rows"], ltri]
    return pl.pallas_call(
        _proj_kernel,
        grid=(batch, seq // ts),
        in_specs=[row_tile(D_MODEL),
                  pl.BlockSpec((1, 1, 3 * D_MODEL), lambda b, s: (b, 0, 0)),
                  _resident((1, D_MODEL))]
                 + [_resident(w.shape) for w in weights]
                 + [row_tile(LANES), row_tile(LANES)],
        out_specs=[row_tile(HEAD_BLOCKS)] * 2 + [vt_tile, row_tile(FOX_W)]
                  + [row_tile(HEAD_BLOCKS)] * 2 + [vt_tile, row_tile(FOX_W)],
        out_shape=[wide, wide, wide_t, narrow, wide, wide, wide_t, narrow],
        scratch_shapes=[pltpu.VMEM((8, LANES), F32)],
        compiler_params=pltpu.CompilerParams(
            dimension_semantics=("arbitrary", "arbitrary"), vmem_limit_bytes=VMEM_LIMIT),
        name="in_proj",
    )(x, mod_l.reshape(batch, 1, 3 * D_MODEL), norm_g_l.reshape(1, D_MODEL), *weights, cos_t, sin_t)


def _attn_kernel(q_ref, k_ref, vt_ref, sg_ref, o_ref, m_ref, acc_ref, ahead_ref, *, chunk):
    tq = q_ref.shape[1]
    i = pl.program_id(1)
    m_ref[...] = jnp.full(m_ref.shape, NEG_BIG, F32)
    acc_ref[...] = jnp.zeros(acc_ref.shape, F32)

    half = tq // 2
    units = [(h, part) for h in range(HEADS) for part in range(2)]

    def diag_mask(nkeys, part):
        key = lax.broadcasted_iota(jnp.int32, (nkeys, half), 0)
        qry = lax.broadcasted_iota(jnp.int32, (nkeys, half), 1) + part * half
        if chunk == 1:
            return key <= qry
        return (key // chunk) <= (qry // chunk)

    def scores(j, u, nkeys):
        h, part = units[u]
        start = pl.multiple_of(j * tq, tq)
        blk = slice(h * LANES, (h + 1) * LANES)
        return _dot_nt(k_ref[0, pl.ds(start, nkeys), blk],
                       q_ref[0, part * half:(part + 1) * half, blk])

    ring = ahead_ref.shape[0]
    assert len(units) % ring == 0 and ring > SCORE_LOOKAHEAD

    def kv_step(j, masked):
        def unit_keys(u):
            return half if (masked and units[u][1] == 0 and u >= SCORE_LOOKAHEAD) else tq

        for u, (h, part) in enumerate(units):
            cols = slice(part * half, (part + 1) * half)
            nkeys = unit_keys(u)
            ahead = u + SCORE_LOOKAHEAD
            if ahead < len(units):
                ahead_ref[ahead % ring, 0:unit_keys(ahead), :] = scores(j, ahead, unit_keys(ahead))
            elif not masked:
                ahead_ref[ahead % ring] = scores(j + 1, ahead - len(units), tq)

            st = ahead_ref[u % ring, 0:nkeys, :]
            if masked:
                st = jnp.where(diag_mask(nkeys, part), st, NEG_BIG)
            m_prev = m_ref[h, :, cols]
            m_new = jnp.maximum(m_prev, jnp.max(st, axis=0, keepdims=True))
            alpha = jnp.exp2(m_prev - m_new)
            p = jnp.exp2(st - jnp.tile(m_new, (nkeys // 8, 1)))
            vt = vt_ref[0, j, h * V_ROWS:(h + 1) * V_ROWS, 0:nkeys]
            acc_ref[h, :, cols] = (acc_ref[h, :, cols] * jnp.tile(alpha, (V_ROWS // 8, 1))
                                   + _dot(vt, p.astype(BF16)))
            m_ref[h, :, cols] = m_new

    for s in range(SCORE_LOOKAHEAD):
        ahead_ref[s] = scores(0, s, tq)

    @pl.loop(0, i // 2)
    def _(jj):
        kv_step(2 * jj, False)
        kv_step(2 * jj + 1, False)

    @pl.when(i % 2 == 1)
    def _():
        kv_step(i - 1, False)

    kv_step(i, True)

    for pair in range(HEADS // 2):
        halves = []
        for h in (2 * pair, 2 * pair + 1):
            a = acc_ref[h]
            halves.append(a[0:HEAD_DIM, :] / a[HEAD_DIM:HEAD_DIM + 1, :])
        out = jnp.concatenate(halves, axis=0).T
        sg = sg_ref[0, :, pair * LANES:(pair + 1) * LANES].astype(F32)
        o_ref[0, :, pair * LANES:(pair + 1) * LANES] = (out * sg).astype(o_ref.dtype)


def _attention(q, k, vt, sg, chunk):
    batch, seq, _ = q.shape
    tq = Q_TILE
    assert vt.shape == (batch, seq // tq, HEADS * V_ROWS, tq)
    return pl.pallas_call(
        functools.partial(_attn_kernel, chunk=chunk),
        grid=(batch, seq // tq),
        in_specs=[
            pl.BlockSpec((1, tq, HEAD_BLOCKS), lambda b, i: (b, i, 0)),
            pl.BlockSpec((1, seq, HEAD_BLOCKS), lambda b, i: (b, 0, 0)),
            pl.BlockSpec((1, seq // tq, HEADS * V_ROWS, tq), lambda b, i: (b, 0, 0, 0)),
            pl.BlockSpec((1, tq, FOX_W), lambda b, i: (b, i, 0)),
        ],
        out_specs=pl.BlockSpec((1, tq, FOX_W), lambda b, i: (b, i, 0)),
        out_shape=jax.ShapeDtypeStruct((batch, seq, FOX_W), BF16),
        scratch_shapes=[pltpu.VMEM((HEADS, 8, tq), F32), pltpu.VMEM((HEADS, V_ROWS, tq), F32),
                        pltpu.VMEM((SCORE_LOOKAHEAD + 1, tq, tq // 2), F32)],
        compiler_params=pltpu.CompilerParams(
            dimension_semantics=("arbitrary", "arbitrary"), vmem_limit_bytes=VMEM_LIMIT),
        name="attn_fox" if chunk == 1 else "attn_mla",
    )(q, k, vt, sg)


def _out_kernel(x_ref, yf_ref, ym_ref, w_ref, mod_ref, fg_ref, o_ref, *, final):
    y = _dot(yf_ref[0], w_ref[0:FOX_W, :]) + _dot(ym_ref[0], w_ref[FOX_W:2 * FOX_W, :])
    gate = mod_ref[0, :, 2 * D_MODEL:3 * D_MODEL]
    xn = x_ref[0] + gate * y
    if final:
        xn = _rms(xn, fg_ref[...])
    o_ref[0] = xn


def _out_project(x, yf, ym, wout, mod_l, final_g, final):
    batch, seq, _ = x.shape
    ts = SEQ_TILE
    row_tile = lambda w: pl.BlockSpec((1, ts, w), lambda b, s: (b, s, 0))
    return pl.pallas_call(
        functools.partial(_out_kernel, final=final),
        grid=(batch, seq // ts),
        in_specs=[row_tile(D_MODEL), row_tile(FOX_W), row_tile(FOX_W),
                  _resident((D_MODEL, D_MODEL)),
                  pl.BlockSpec((1, 1, 3 * D_MODEL), lambda b, s: (b, 0, 0)),
                  _resident((1, D_MODEL))],
        out_specs=row_tile(D_MODEL),
        out_shape=jax.ShapeDtypeStruct(x.shape, F32),
        compiler_params=pltpu.CompilerParams(
            dimension_semantics=("arbitrary", "arbitrary"), vmem_limit_bytes=VMEM_LIMIT),
        name="out_proj_final" if final else "out_proj",
    )(x, yf, ym, wout, mod_l.reshape(batch, 1, 3 * D_MODEL), final_g.reshape(1, D_MODEL))


def kernel(x, c, positions, norm_g, w_ada, b_ada, w_in, b_f, q_norm_g, w_uq, kv_norm_g, w_ukv,
           w_out, final_g):
    mod = _adaln(c, w_ada, b_ada)
    cos_t, sin_t = _rope_tables(positions)
    eq, ek = _aug_matrices()
    ltri = np.tril(np.ones((SEQ_TILE, SEQ_TILE), np.float32))
    consts = (jnp.asarray(eq, BF16), jnp.asarray(ek, BF16), jnp.asarray(ltri, BF16))
    for l in range(DEPTH):
        lw = _prep_layer(w_in[l], w_uq[l], w_ukv[l], w_out[l], b_f[l], q_norm_g[l], kv_norm_g[l])
        qf, kf, vf, sgf, qm, km, vm, sgm = _project(x, mod[l], norm_g[l], lw, consts, cos_t, sin_t)
        yf = _attention(qf, kf, vf, sgf, chunk=1)
        ym = _attention(qm, km, vm, sgm, chunk=CHUNK)
        x = _out_project(x, yf, ym, lw["wout"], mod[l], final_g, final=(l == DEPTH - 1))
    return x
```

```python
import functools

import numpy as np
import jax
import jax.numpy as jnp
from jax import lax
from jax.experimental import pallas as pl
from jax.experimental.pallas import tpu as pltpu

D_MODEL = 1024
DEPTH = 4
CHUNK = 64
HEADS = 8
HEAD_DIM = 64
FOX_W = HEADS * HEAD_DIM
MLA_ROPE = 32
MLA_QK = HEAD_DIM + MLA_ROPE
Q_LORA = 256
KV_LORA = 128
ROPE_THETA = 10000.0
EPS = 1e-6

LANES = 128
HEAD_BLOCKS = HEADS * LANES
LOG2E = 1.4426950408889634
NEG_BIG = -1e30

SEQ_TILE = 512
Q_TILE = 512
SCORE_LOOKAHEAD = 3
SCORE_RING = 4
V_ROWS = 80
VMEM_LIMIT = 56 * 1024 * 1024


R_MISC, R_FG, R_QL, R_KVL, R_MG, R_END = 0, 128, 640, 896, 1024, 1536

ROW_QONES, ROW_KONES, ROW_BF, ROW_QG, ROW_KVG, ROW_ROPE = 0, 1, 3, 4, 5, 6

F32 = jnp.float32
BF16 = jnp.bfloat16


def _dot(a, b):
    return jnp.dot(a, b, preferred_element_type=F32)


def _dot_nt(a, b):
    return lax.dot_general(a, b, (((1,), (1,)), ((), ())), preferred_element_type=F32)


def _split3(v):
    hi = v.astype(BF16)
    r1 = v - hi.astype(F32)
    mid = r1.astype(BF16)
    lo = (r1 - mid.astype(F32)).astype(BF16)
    return hi, mid, lo


def _rms(v, g):
    return v * lax.rsqrt(jnp.mean(v * v, axis=-1, keepdims=True) + EPS) * g


def _silu(v):
    return v * jax.nn.sigmoid(v)


def _params():
    return pltpu.CompilerParams(
        dimension_semantics=("arbitrary", "arbitrary"), vmem_limit_bytes=VMEM_LIMIT)


def _adaln_kernel(c_ref, w_ref, b_ref, o_ref):
    cv = c_ref[...]
    ca = (cv * jax.nn.sigmoid(cv)).astype(BF16)
    o_ref[0] = _dot(ca, w_ref[0].astype(BF16)) + b_ref[0]


def _adaln(c, w_ada, b_ada):
    batch = c.shape[0]
    n_tiles = 3
    return pl.pallas_call(
        _adaln_kernel,
        grid=(DEPTH, n_tiles),
        in_specs=[
            pl.BlockSpec((batch, D_MODEL), lambda l, n: (0, 0)),
            pl.BlockSpec((1, D_MODEL, D_MODEL), lambda l, n: (l, 0, n)),
            pl.BlockSpec((1, 1, D_MODEL), lambda l, n: (l, 0, n)),
        ],
        out_specs=pl.BlockSpec((1, batch, D_MODEL), lambda l, n: (l, 0, n)),
        out_shape=jax.ShapeDtypeStruct((DEPTH, batch, 3 * D_MODEL), F32),
        compiler_params=_params(),
        name="adaln_mod",
    )(c, w_ada, b_ada.reshape(DEPTH, 1, 3 * D_MODEL))


def _rope_table_kernel(pos_ref, invf_ref, cos_ref, sin_lo_ref, sin_hi_ref):
    ang = pos_ref[0].astype(F32) * invf_ref[...]
    sin = jnp.sin(ang)
    lane = lax.broadcasted_iota(jnp.int32, ang.shape, 1)
    first = lane < HEAD_DIM + MLA_ROPE // 2
    cos_ref[0] = jnp.cos(ang)
    sin_lo_ref[0] = jnp.where(first, -sin, 0.0)
    sin_hi_ref[0] = jnp.where(first, 0.0, sin)


def _rope_tables(positions):
    batch, seq = positions.shape
    inv_freq = 1.0 / (ROPE_THETA ** (jnp.arange(0, MLA_ROPE, 2, dtype=F32) / MLA_ROPE))
    half = MLA_ROPE // 2
    invf_row = jnp.zeros((1, LANES), F32)
    invf_row = invf_row.at[0, HEAD_DIM:HEAD_DIM + half].set(inv_freq)
    invf_row = invf_row.at[0, HEAD_DIM + half:HEAD_DIM + MLA_ROPE].set(inv_freq)
    ts = SEQ_TILE
    out = jax.ShapeDtypeStruct((batch, seq, LANES), F32)
    return pl.pallas_call(
        _rope_table_kernel,
        grid=(batch, seq // ts),
        in_specs=[
            pl.BlockSpec((1, ts, 1), lambda b, s: (b, s, 0)),
            pl.BlockSpec((1, LANES), lambda b, s: (0, 0)),
        ],
        out_specs=[pl.BlockSpec((1, ts, LANES), lambda b, s: (b, s, 0))] * 3,
        out_shape=[out, out, out],
        compiler_params=_params(),
        name="rope_tables",
    )(positions.reshape(batch, seq, 1), invf_row)


def _fox_aug_lane(h):
    return h * LANES + (HEAD_DIM if h % 2 == 0 else 0)


def _const_rows():
    rows = np.zeros((8, HEAD_BLOCKS), np.float32)
    for h in range(HEADS):
        base = _fox_aug_lane(h)
        rows[ROW_QONES, base + 3:base + 6] = 1.0
        rows[ROW_KONES, base:base + 3] = 1.0
    rows[ROW_ROPE, HEAD_DIM:MLA_QK] = 1.0
    return rows


def _aug_matrices():
    eq = np.zeros((LANES, HEAD_BLOCKS), np.float32)
    ek = np.zeros((LANES, HEAD_BLOCKS), np.float32)
    for p in range(3):
        for h in range(HEADS):
            eq[p * HEADS + h, _fox_aug_lane(h) + p] = 1.0
            ek[p * HEADS + h, _fox_aug_lane(h) + 3 + p] = -1.0
    return eq, ek


def _prep_weights(w_in, w_uq, w_ukv, w_out, b_f, q_norm_g, kv_norm_g):
    o = np.cumsum([0, FOX_W, FOX_W, FOX_W, HEADS, FOX_W, Q_LORA, KV_LORA, MLA_ROPE, FOX_W])
    w_fq, w_fk, w_fv, w_ff, w_fg, w_ql, w_kvl, w_kr, w_mg = (
        w_in[:, :, o[i]:o[i + 1]] for i in range(9))
    misc = jnp.zeros((DEPTH, D_MODEL, LANES), F32)
    misc = misc.at[:, :, 0:HEADS].set(w_ff).at[:, :, HEAD_DIM:MLA_QK].set(w_kr)
    wr = jnp.concatenate([misc, w_fg, w_ql, w_kvl, w_mg], axis=2).astype(BF16)

    uq = w_uq.reshape(DEPTH, Q_LORA, HEADS, MLA_QK)
    wuq = jnp.pad(uq, ((0, 0), (0, 0), (0, 0), (0, LANES - MLA_QK)))
    ukv = w_ukv.reshape(DEPTH, KV_LORA, HEADS, 2 * HEAD_DIM)
    wkk = jnp.pad(ukv[..., :HEAD_DIM], ((0, 0), (0, 0), (0, 0), (0, LANES - HEAD_DIM)))
    wkvt = jnp.swapaxes(ukv[..., HEAD_DIM:].reshape(DEPTH, KV_LORA, FOX_W), 1, 2)

    rows = jnp.broadcast_to(jnp.asarray(_const_rows()), (DEPTH, 8, HEAD_BLOCKS))
    rows = rows.at[:, ROW_BF, :HEADS].set(b_f)
    rows = rows.at[:, ROW_QG, :Q_LORA].set(q_norm_g)
    rows = rows.at[:, ROW_KVG, :KV_LORA].set(kv_norm_g)
    return dict(
        wq=w_fq.astype(BF16), wk=w_fk.astype(BF16),
        wvt=jnp.swapaxes(w_fv, 1, 2).astype(BF16),
        wr=wr,
        wuq=wuq.reshape(DEPTH, Q_LORA, HEAD_BLOCKS).astype(BF16),
        wkk=wkk.reshape(DEPTH, KV_LORA, HEAD_BLOCKS).astype(BF16),
        wkvt=wkvt.astype(BF16),
        rows=rows, wout=w_out.astype(BF16))


def _proj_kernel(*refs, fused):
    if fused:
        x_ref, yf_ref, ym_ref, wout_ref, modp_ref, *refs = refs
    else:
        x_ref, *refs = refs
    (mod_ref, g_ref, wq_ref, wk_ref, wvt_ref, wr_ref, wuq_ref, wkk_ref, wkvt_ref, rows_ref,
     eq_ref, ek_ref, ltri_ref, cos_ref, sin_lo_ref, sin_hi_ref, *outs) = refs
    if fused:
        xo_ref, *outs = outs
    qf_ref, kf_ref, vft_ref, sgf_ref, qm_ref, km_ref, vmt_ref, sgm_ref, carry_ref = outs

    ts = x_ref.shape[1]
    pad_rows = lax.broadcasted_iota(jnp.int32, (V_ROWS - HEAD_DIM, ts), 0)
    ones_then_zeros = jnp.where(pad_rows == 0, 1.0, 0.0).astype(BF16)

    def store_value_blocks(out_ref, vt):
        for h in range(HEADS):
            out_ref[0, 0, h * V_ROWS:h * V_ROWS + HEAD_DIM, :] = (
                vt[h * HEAD_DIM:(h + 1) * HEAD_DIM, :].astype(BF16))
            out_ref[0, 0, h * V_ROWS + HEAD_DIM:(h + 1) * V_ROWS, :] = ones_then_zeros

    def rope(t, cos, sin_lo, sin_hi):
        width = t.shape[1]
        down = pltpu.roll(t, width - MLA_ROPE // 2, 1)
        up = pltpu.roll(t, MLA_ROPE // 2, 1)
        return t * cos + down * sin_lo + up * sin_hi

    @pl.when(pl.program_id(1) == 0)
    def _():
        carry_ref[...] = jnp.zeros_like(carry_ref)

    xs = x_ref[0]
    if fused:
        y = (_dot(yf_ref[0], wout_ref[0:FOX_W, :]) + _dot(ym_ref[0], wout_ref[FOX_W:2 * FOX_W, :]))
        xs = xs + modp_ref[0, :, 2 * D_MODEL:3 * D_MODEL] * y
        xo_ref[0] = xs
    shift = mod_ref[0, :, 0:D_MODEL]
    scale = mod_ref[0, :, D_MODEL:2 * D_MODEL]
    h = _rms(xs, g_ref[...]) * (1.0 + scale) + shift
    hb = h.astype(BF16)

    zr = _dot(hb, wr_ref[...])
    misc = zr[:, R_MISC:R_MISC + LANES]
    q_scale = HEAD_DIM ** -0.5 * LOG2E
    zq = _dot(hb, wq_ref[...]) * q_scale

    ff = misc + rows_ref[ROW_BF:ROW_BF + 1, 0:LANES]
    lf = jnp.minimum(ff, 0.0) - jnp.log1p(jnp.exp(-jnp.abs(ff)))
    hi, mid, lo = _split3(lf)
    ltri = ltri_ref[...]
    cum = _dot(ltri, hi) + _dot(ltri, mid) + _dot(ltri, lo) + carry_ref[0:1, :]
    carry_ref[...] = jnp.broadcast_to(cum[ts - 1:ts, :], carry_ref.shape)

    zk = _dot(hb, wk_ref[...])

    cos, sin_lo, sin_hi = cos_ref[0], sin_lo_ref[0], sin_hi_ref[0]
    tile8 = lambda t: jnp.concatenate([t] * HEADS, axis=1)
    qn = _rms(zr[:, R_QL:R_QL + Q_LORA], rows_ref[ROW_QG:ROW_QG + 1, 0:Q_LORA]).astype(BF16)
    m_scale = MLA_QK ** -0.5 * LOG2E
    qm = rope(_dot(qn, wuq_ref[...]), tile8(cos), tile8(sin_lo), tile8(sin_hi)) * m_scale
    qm_ref[0] = qm.astype(BF16)
    kvn = _rms(zr[:, R_KVL:R_KVL + KV_LORA], rows_ref[ROW_KVG:ROW_KVG + 1, 0:KV_LORA]).astype(BF16)
    krope = rope(misc * rows_ref[ROW_ROPE:ROW_ROPE + 1, 0:LANES], cos, sin_lo, sin_hi)
    km_ref[0] = (_dot(kvn, wkk_ref[...]) + tile8(krope)).astype(BF16)
    store_value_blocks(vmt_ref, _dot_nt(wkvt_ref[...], kvn))
    store_value_blocks(vft_ref, _dot_nt(wvt_ref[...], hb))
    sgf_ref[0] = _silu(zr[:, R_FG:R_FG + FOX_W]).astype(BF16)
    sgm_ref[0] = _silu(zr[:, R_MG:R_MG + FOX_W]).astype(BF16)

    chi, cmid, clo = _split3(cum * LOG2E)
    lane = lax.broadcasted_iota(jnp.int32, (ts, LANES), 1)
    cpieces = jnp.where(
        lane < HEADS, chi.astype(F32),
        jnp.where(lane < 2 * HEADS, pltpu.roll(cmid.astype(F32), HEADS, 1),
                  jnp.where(lane < 3 * HEADS, pltpu.roll(clo.astype(F32), 2 * HEADS, 1), 0.0)))
    cb = cpieces.astype(BF16)
    aug_q = _dot(cb, eq_ref[...]) + rows_ref[ROW_QONES:ROW_QONES + 1, :]
    aug_k = _dot(cb, ek_ref[...]) + rows_ref[ROW_KONES:ROW_KONES + 1, :]
    for hd in range(HEADS):
        pair = slice((hd // 2) * LANES, (hd // 2 + 1) * LANES)
        blk = slice(hd * LANES, (hd + 1) * LANES)
        own = (lane < HEAD_DIM) if hd % 2 == 0 else (lane >= HEAD_DIM)
        qf_ref[0, :, blk] = jnp.where(own, zq[:, pair], aug_q[:, blk]).astype(BF16)
        kf_ref[0, :, blk] = jnp.where(own, zk[:, pair], aug_k[:, blk]).astype(BF16)


def _resident(shape):
    nd = len(shape)
    return pl.BlockSpec(shape, lambda b, s: (0,) * nd, pipeline_mode=pl.Buffered(1))


def _layer_slab(stacked, layer):
    nd = stacked.ndim - 1
    return pl.BlockSpec((None,) + stacked.shape[1:], lambda b, s: (layer,) + (0,) * nd,
                        pipeline_mode=pl.Buffered(1))


def _mod_spec(layer):
    return pl.BlockSpec((None, 1, 1, 3 * D_MODEL), lambda b, s: (layer, b, 0, 0))


def _project(x, prev, layer, mod4, norm_g3, w, consts, tables):
    batch, seq, _ = x.shape
    ts = SEQ_TILE
    fused = prev is not None
    row_tile = lambda width: pl.BlockSpec((1, ts, width), lambda b, s: (b, s, 0))
    wide = jax.ShapeDtypeStruct((batch, seq, HEAD_BLOCKS), BF16)
    narrow = jax.ShapeDtypeStruct((batch, seq, FOX_W), BF16)
    wide_t = jax.ShapeDtypeStruct((batch, seq // ts, HEADS * V_ROWS, ts), BF16)
    vt_tile = pl.BlockSpec((1, 1, HEADS * V_ROWS, ts), lambda b, s: (b, s, 0, 0))
    layer_weights = [w[name] for name in ("wq", "wk", "wvt", "wr", "wuq", "wkk", "wkvt", "rows")]

    args, in_specs = [x], [row_tile(D_MODEL)]
    if fused:
        args += [prev[0], prev[1], w["wout"], mod4]
        in_specs += [row_tile(FOX_W), row_tile(FOX_W), _layer_slab(w["wout"], layer - 1),
                     _mod_spec(layer - 1)]
    args += [mod4, norm_g3] + layer_weights + list(consts) + list(tables)
    in_specs += ([_mod_spec(layer), _layer_slab(norm_g3, layer)]
                 + [_layer_slab(a, layer) for a in layer_weights]
                 + [_resident(a.shape) for a in consts]
                 + [row_tile(LANES)] * 3)
    out_specs = ([row_tile(HEAD_BLOCKS)] * 2 + [vt_tile, row_tile(FOX_W)]
                 + [row_tile(HEAD_BLOCKS)] * 2 + [vt_tile, row_tile(FOX_W)])
    out_shape = [wide, wide, wide_t, narrow, wide, wide, wide_t, narrow]
    if fused:
        out_specs = [row_tile(D_MODEL)] + out_specs
        out_shape = [jax.ShapeDtypeStruct(x.shape, F32)] + out_shape
    outs = pl.pallas_call(
        functools.partial(_proj_kernel, fused=fused),
        grid=(batch, seq // ts),
        in_specs=in_specs, out_specs=out_specs, out_shape=out_shape,
        scratch_shapes=[pltpu.VMEM((8, LANES), F32)],
        compiler_params=_params(),
        name="out_in_proj" if fused else "in_proj",
    )(*args)
    return (outs[0], outs[1:]) if fused else (x, outs)


def _attn_kernel(q_ref, k_ref, vt_ref, sg_ref, o_ref, m_ref, acc_ref, ahead_ref, *, chunk):
    tq = q_ref.shape[1]
    i = pl.program_id(1)
    m_ref[...] = jnp.full(m_ref.shape, NEG_BIG, F32)
    acc_ref[...] = jnp.zeros(acc_ref.shape, F32)

    half = tq // 2
    units = [(h, part) for h in range(HEADS) for part in range(2)]

    def diag_mask(nkeys, part):
        key = lax.broadcasted_iota(jnp.int32, (nkeys, half), 0)
        qry = lax.broadcasted_iota(jnp.int32, (nkeys, half), 1) + part * half
        if chunk == 1:
            return key <= qry
        return (key // chunk) <= (qry // chunk)

    def scores(j, u, nkeys):
        h, part = units[u]
        start = pl.multiple_of(j * tq, tq)
        blk = slice(h * LANES, (h + 1) * LANES)
        return _dot_nt(k_ref[0, pl.ds(start, nkeys), blk],
                       q_ref[0, part * half:(part + 1) * half, blk])

    ring = ahead_ref.shape[0]
    assert len(units) % ring == 0 and ring > SCORE_LOOKAHEAD

    def kv_step(j, masked):
        def unit_keys(u):
            return half if (masked and units[u][1] == 0 and u >= SCORE_LOOKAHEAD) else tq

        for u, (h, part) in enumerate(units):
            cols = slice(part * half, (part + 1) * half)
            nkeys = unit_keys(u)
            ahead = u + SCORE_LOOKAHEAD
            if ahead < len(units):
                ahead_ref[ahead % ring, 0:unit_keys(ahead), :] = scores(j, ahead, unit_keys(ahead))
            elif not masked:
                ahead_ref[ahead % ring] = scores(j + 1, ahead - len(units), tq)

            st = ahead_ref[u % ring, 0:nkeys, :]
            if masked:
                st = jnp.where(diag_mask(nkeys, part), st, NEG_BIG)
            m_prev = m_ref[h, :, cols]
            m_new = jnp.maximum(m_prev, jnp.max(st, axis=0, keepdims=True))
            alpha = jnp.exp2(m_prev - m_new)
            p = jnp.exp2(st - jnp.tile(m_new, (nkeys // 8, 1)))
            vt = vt_ref[0, j, h * V_ROWS:(h + 1) * V_ROWS, 0:nkeys]
            acc_ref[h, :, cols] = (acc_ref[h, :, cols] * jnp.tile(alpha, (V_ROWS // 8, 1))
                                   + _dot(vt, p.astype(BF16)))
            m_ref[h, :, cols] = m_new

    for s in range(SCORE_LOOKAHEAD):
        ahead_ref[s] = scores(0, s, tq)

    @pl.loop(0, i // 2)
    def _(jj):
        kv_step(2 * jj, False)
        kv_step(2 * jj + 1, False)

    @pl.when(i % 2 == 1)
    def _():
        kv_step(i - 1, False)

    kv_step(i, True)

    for pair in range(HEADS // 2):
        halves = []
        for h in (2 * pair, 2 * pair + 1):
            a = acc_ref[h]
            halves.append(a[0:HEAD_DIM, :] / a[HEAD_DIM:HEAD_DIM + 1, :])
        out = jnp.concatenate(halves, axis=0).T
        sg = sg_ref[0, :, pair * LANES:(pair + 1) * LANES].astype(F32)
        o_ref[0, :, pair * LANES:(pair + 1) * LANES] = (out * sg).astype(o_ref.dtype)


def _attention(q, k, vt, sg, chunk):
    batch, seq, _ = q.shape
    tq = Q_TILE
    assert vt.shape == (batch, seq // tq, HEADS * V_ROWS, tq)
    return pl.pallas_call(
        functools.partial(_attn_kernel, chunk=chunk),
        grid=(batch, seq // tq),
        in_specs=[
            pl.BlockSpec((1, tq, HEAD_BLOCKS), lambda b, i: (b, i, 0)),
            pl.BlockSpec((1, seq, HEAD_BLOCKS), lambda b, i: (b, 0, 0)),
            pl.BlockSpec((1, seq // tq, HEADS * V_ROWS, tq), lambda b, i: (b, 0, 0, 0)),
            pl.BlockSpec((1, tq, FOX_W), lambda b, i: (b, i, 0)),
        ],
        out_specs=pl.BlockSpec((1, tq, FOX_W), lambda b, i: (b, i, 0)),
        out_shape=jax.ShapeDtypeStruct((batch, seq, FOX_W), BF16),
        scratch_shapes=[pltpu.VMEM((HEADS, 8, tq), F32), pltpu.VMEM((HEADS, V_ROWS, tq), F32),
                        pltpu.VMEM((SCORE_RING, tq, tq // 2), F32)],
        compiler_params=_params(),
        name="attn_fox" if chunk == 1 else "attn_mla",
    )(q, k, vt, sg)


def _final_kernel(x_ref, yf_ref, ym_ref, w_ref, mod_ref, fg_ref, o_ref):
    y = _dot(yf_ref[0], w_ref[0:FOX_W, :]) + _dot(ym_ref[0], w_ref[FOX_W:2 * FOX_W, :])
    gate = mod_ref[0, :, 2 * D_MODEL:3 * D_MODEL]
    o_ref[0] = _rms(x_ref[0] + gate * y, fg_ref[...])


def _final_project(x, yf, ym, wout, mod4, final_g):
    batch, seq, _ = x.shape
    ts = SEQ_TILE
    row_tile = lambda width: pl.BlockSpec((1, ts, width), lambda b, s: (b, s, 0))
    return pl.pallas_call(
        _final_kernel,
        grid=(batch, seq // ts),
        in_specs=[row_tile(D_MODEL), row_tile(FOX_W), row_tile(FOX_W),
                  _layer_slab(wout, DEPTH - 1), _mod_spec(DEPTH - 1), _resident((1, D_MODEL))],
        out_specs=row_tile(D_MODEL),
        out_shape=jax.ShapeDtypeStruct(x.shape, F32),
        compiler_params=_params(),
        name="out_proj_final",
    )(x, yf, ym, wout, mod4, final_g.reshape(1, D_MODEL))


def kernel(x, c, positions, norm_g, w_ada, b_ada, w_in, b_f, q_norm_g, w_uq, kv_norm_g, w_ukv,
           w_out, final_g):
    batch = x.shape[0]
    mod4 = _adaln(c, w_ada, b_ada).reshape(DEPTH, batch, 1, 3 * D_MODEL)
    tables = _rope_tables(positions)
    eq, ek = _aug_matrices()
    ltri = np.tril(np.ones((SEQ_TILE, SEQ_TILE), np.float32))
    consts = (jnp.asarray(eq, BF16), jnp.asarray(ek, BF16), jnp.asarray(ltri, BF16))
    w = _prep_weights(w_in, w_uq, w_ukv, w_out, b_f, q_norm_g, kv_norm_g)
    norm_g3 = norm_g.reshape(DEPTH, 1, D_MODEL)
    prev = None
    for layer in range(DEPTH):
        x, (qf, kf, vf, sgf, qm, km, vm, sgm) = _project(
            x, prev, layer, mod4, norm_g3, w, consts, tables)
        prev = (_attention(qf, kf, vf, sgf, chunk=1), _attention(qm, km, vm, sgm, chunk=CHUNK))
    return _final_project(x, prev[0], prev[1], w["wout"], mod4, final_g)
```

```python
import functools

import numpy as np
import jax
import jax.numpy as jnp
from jax import lax
from jax.experimental import pallas as pl
from jax.experimental.pallas import tpu as pltpu

D_MODEL = 1024
DEPTH = 4
CHUNK = 64
HEADS = 8
HEAD_DIM = 64
FOX_W = HEADS * HEAD_DIM
MLA_ROPE = 32
MLA_QK = HEAD_DIM + MLA_ROPE
Q_LORA = 256
KV_LORA = 128
ROPE_THETA = 10000.0
EPS = 1e-6

LANES = 128
HEAD_BLOCKS = HEADS * LANES
LOG2E = 1.4426950408889634
NEG_BIG = -1e30

SEQ_TILE = 512
Q_TILE = 512
SCORE_LOOKAHEAD = 3
SCORE_RING = 4
V_ROWS = 80
VMEM_LIMIT = 56 * 1024 * 1024


R_MISC, R_FG, R_QL, R_KVL, R_MG, R_END = 0, 128, 640, 896, 1024, 1536

ROW_KONES, ROW_BF, ROW_QG, ROW_KVG, ROW_ROPE = 1, 3, 4, 5, 6

F32 = jnp.float32
BF16 = jnp.bfloat16


def _dot(a, b):
    return jnp.dot(a, b, preferred_element_type=F32)


def _dot_nt(a, b):
    return lax.dot_general(a, b, (((1,), (1,)), ((), ())), preferred_element_type=F32)


def _split3(v):
    hi = v.astype(BF16)
    r1 = v - hi.astype(F32)
    mid = r1.astype(BF16)
    lo = (r1 - mid.astype(F32)).astype(BF16)
    return hi, mid, lo


def _rms(v, g):
    return v * lax.rsqrt(jnp.mean(v * v, axis=-1, keepdims=True) + EPS) * g


def _silu(v):
    return v * jax.nn.sigmoid(v)


def _params():
    return pltpu.CompilerParams(
        dimension_semantics=("arbitrary", "arbitrary"), vmem_limit_bytes=VMEM_LIMIT)


def _adaln_kernel(c_ref, w_ref, b_ref, o_ref):
    cv = c_ref[...]
    ca = (cv * jax.nn.sigmoid(cv)).astype(BF16)
    o_ref[0] = _dot(ca, w_ref[0].astype(BF16)) + b_ref[0]


def _adaln(c, w_ada, b_ada):
    batch = c.shape[0]
    n_tiles = 3
    return pl.pallas_call(
        _adaln_kernel,
        grid=(DEPTH, n_tiles),
        in_specs=[
            pl.BlockSpec((batch, D_MODEL), lambda l, n: (0, 0)),
            pl.BlockSpec((1, D_MODEL, D_MODEL), lambda l, n: (l, 0, n)),
            pl.BlockSpec((1, 1, D_MODEL), lambda l, n: (l, 0, n)),
        ],
        out_specs=pl.BlockSpec((1, batch, D_MODEL), lambda l, n: (l, 0, n)),
        out_shape=jax.ShapeDtypeStruct((DEPTH, batch, 3 * D_MODEL), F32),
        compiler_params=_params(),
        name="adaln_mod",
    )(c, w_ada, b_ada.reshape(DEPTH, 1, 3 * D_MODEL))


def _rope_table_kernel(pos_ref, invf_ref, cos_ref, sin_lo_ref, sin_hi_ref):
    ang = pos_ref[0].astype(F32) * invf_ref[...]
    sin = jnp.sin(ang)
    lane = lax.broadcasted_iota(jnp.int32, ang.shape, 1)
    first = lane < HEAD_DIM + MLA_ROPE // 2
    cos_ref[0] = jnp.cos(ang)
    sin_lo_ref[0] = jnp.where(first, -sin, 0.0)
    sin_hi_ref[0] = jnp.where(first, 0.0, sin)


def _rope_tables(positions):
    batch, seq = positions.shape
    inv_freq = 1.0 / (ROPE_THETA ** (jnp.arange(0, MLA_ROPE, 2, dtype=F32) / MLA_ROPE))
    half = MLA_ROPE // 2
    invf_row = jnp.zeros((1, LANES), F32)
    invf_row = invf_row.at[0, HEAD_DIM:HEAD_DIM + half].set(inv_freq)
    invf_row = invf_row.at[0, HEAD_DIM + half:HEAD_DIM + MLA_ROPE].set(inv_freq)
    ts = SEQ_TILE
    out = jax.ShapeDtypeStruct((batch, seq, LANES), F32)
    return pl.pallas_call(
        _rope_table_kernel,
        grid=(batch, seq // ts),
        in_specs=[
            pl.BlockSpec((1, ts, 1), lambda b, s: (b, s, 0)),
            pl.BlockSpec((1, LANES), lambda b, s: (0, 0)),
        ],
        out_specs=[pl.BlockSpec((1, ts, LANES), lambda b, s: (b, s, 0))] * 3,
        out_shape=[out, out, out],
        compiler_params=_params(),
        name="rope_tables",
    )(positions.reshape(batch, seq, 1), invf_row)


def _fox_aug_lane(h):
    return h * LANES + (HEAD_DIM if h % 2 == 0 else 0)


def _const_rows():
    rows = np.zeros((8, HEAD_BLOCKS), np.float32)
    for h in range(HEADS):
        base = _fox_aug_lane(h)
        rows[ROW_KONES, base:base + 3] = 1.0
    rows[ROW_ROPE, HEAD_DIM:MLA_QK] = 1.0
    return rows


def _q_ones_rows():
    col = np.zeros((HEAD_BLOCKS, LANES), np.float32)
    for h in range(HEADS):
        base = _fox_aug_lane(h)
        col[base + 3:base + 6, :] = 1.0
    return col


def _aug_matrices():
    eq = np.zeros((LANES, HEAD_BLOCKS), np.float32)
    ek = np.zeros((LANES, HEAD_BLOCKS), np.float32)
    for p in range(3):
        for h in range(HEADS):
            eq[p * HEADS + h, _fox_aug_lane(h) + p] = 1.0
            ek[p * HEADS + h, _fox_aug_lane(h) + 3 + p] = -1.0
    return eq, ek


def _prep_weights(w_in, w_uq, w_ukv, w_out, b_f, q_norm_g, kv_norm_g):
    o = np.cumsum([0, FOX_W, FOX_W, FOX_W, HEADS, FOX_W, Q_LORA, KV_LORA, MLA_ROPE, FOX_W])
    w_fq, w_fk, w_fv, w_ff, w_fg, w_ql, w_kvl, w_kr, w_mg = (
        w_in[:, :, o[i]:o[i + 1]] for i in range(9))
    misc = jnp.zeros((DEPTH, D_MODEL, LANES), F32)
    misc = misc.at[:, :, 0:HEADS].set(w_ff).at[:, :, HEAD_DIM:MLA_QK].set(w_kr)
    wr = jnp.concatenate([misc, w_fg, w_ql, w_kvl, w_mg], axis=2).astype(BF16)

    uq = w_uq.reshape(DEPTH, Q_LORA, HEADS, MLA_QK)
    wuq = jnp.pad(uq, ((0, 0), (0, 0), (0, 0), (0, LANES - MLA_QK)))
    ukv = w_ukv.reshape(DEPTH, KV_LORA, HEADS, 2 * HEAD_DIM)
    wkk = jnp.pad(ukv[..., :HEAD_DIM], ((0, 0), (0, 0), (0, 0), (0, LANES - HEAD_DIM)))
    wkvt = jnp.swapaxes(ukv[..., HEAD_DIM:].reshape(DEPTH, KV_LORA, FOX_W), 1, 2)

    rows = jnp.broadcast_to(jnp.asarray(_const_rows()), (DEPTH, 8, HEAD_BLOCKS))
    rows = rows.at[:, ROW_BF, :HEADS].set(b_f)
    rows = rows.at[:, ROW_QG, :Q_LORA].set(q_norm_g)
    rows = rows.at[:, ROW_KVG, :KV_LORA].set(kv_norm_g)
    return dict(
        wqt=jnp.swapaxes(w_fq, 1, 2).astype(BF16),
        wk=w_fk.astype(BF16),
        wvt=jnp.swapaxes(w_fv, 1, 2).astype(BF16),
        wr=wr,
        wuqt=jnp.swapaxes(wuq.reshape(DEPTH, Q_LORA, HEAD_BLOCKS), 1, 2).astype(BF16),
        wkk=wkk.reshape(DEPTH, KV_LORA, HEAD_BLOCKS).astype(BF16),
        wkvt=wkvt.astype(BF16),
        rows=rows, wout=w_out.astype(BF16))


def _proj_kernel(*refs, fused):
    if fused:
        x_ref, yf_ref, ym_ref, wout_ref, modp_ref, *refs = refs
    else:
        x_ref, *refs = refs
    (mod_ref, g_ref, wqt_ref, wk_ref, wvt_ref, wr_ref, wuqt_ref, wkk_ref, wkvt_ref, rows_ref,
     eqt_ref, ek_ref, qones_ref, ltri_ref, cos_ref, sin_lo_ref, sin_hi_ref, *outs) = refs
    if fused:
        xo_ref, *outs = outs
    qft_ref, kf_ref, vft_ref, sgf_ref, qmt_ref, km_ref, vmt_ref, sgm_ref, carry_ref = outs

    ts = x_ref.shape[1]
    pad_rows = lax.broadcasted_iota(jnp.int32, (V_ROWS - HEAD_DIM, ts), 0)
    ones_then_zeros = jnp.where(pad_rows == 0, 1.0, 0.0).astype(BF16)

    def store_value_blocks(out_ref, vt):
        for h in range(HEADS):
            out_ref[0, 0, h * V_ROWS:h * V_ROWS + HEAD_DIM, :] = (
                vt[h * HEAD_DIM:(h + 1) * HEAD_DIM, :].astype(BF16))
            out_ref[0, 0, h * V_ROWS + HEAD_DIM:(h + 1) * V_ROWS, :] = ones_then_zeros

    def rope(t, cos, sin_lo, sin_hi):
        width = t.shape[1]
        down = pltpu.roll(t, width - MLA_ROPE // 2, 1)
        up = pltpu.roll(t, MLA_ROPE // 2, 1)
        return t * cos + down * sin_lo + up * sin_hi

    @pl.when(pl.program_id(1) == 0)
    def _():
        carry_ref[...] = jnp.zeros_like(carry_ref)

    xs = x_ref[0]
    if fused:
        y = (_dot(yf_ref[0], wout_ref[0:FOX_W, :]) + _dot(ym_ref[0], wout_ref[FOX_W:2 * FOX_W, :]))
        xs = xs + modp_ref[0, :, 2 * D_MODEL:3 * D_MODEL] * y
        xo_ref[0] = xs
    shift = mod_ref[0, :, 0:D_MODEL]
    scale = mod_ref[0, :, D_MODEL:2 * D_MODEL]
    h = _rms(xs, g_ref[...]) * (1.0 + scale) + shift
    hb = h.astype(BF16)

    zr = _dot(hb, wr_ref[...])
    misc = zr[:, R_MISC:R_MISC + LANES]
    q_scale = HEAD_DIM ** -0.5 * LOG2E
    zqt = _dot_nt(wqt_ref[...], hb) * q_scale

    ff = misc + rows_ref[ROW_BF:ROW_BF + 1, 0:LANES]
    lf = jnp.minimum(ff, 0.0) - jnp.log1p(jnp.exp(-jnp.abs(ff)))
    hi, mid, lo = _split3(lf)
    ltri = ltri_ref[...]
    cum = _dot(ltri, hi) + _dot(ltri, mid) + _dot(ltri, lo) + carry_ref[0:1, :]
    carry_ref[...] = jnp.broadcast_to(cum[ts - 1:ts, :], carry_ref.shape)

    zk = _dot(hb, wk_ref[...])

    cos, sin_lo, sin_hi = cos_ref[0], sin_lo_ref[0], sin_hi_ref[0]
    tile8 = lambda t: jnp.concatenate([t] * HEADS, axis=1)
    qn = _rms(zr[:, R_QL:R_QL + Q_LORA], rows_ref[ROW_QG:ROW_QG + 1, 0:Q_LORA]).astype(BF16)
    m_scale = MLA_QK ** -0.5 * LOG2E
    zt = _dot_nt(wuqt_ref[...], qn)
    rot = MLA_ROPE // 2
    down = jnp.concatenate([zt[rot:], zt[:rot]], axis=0)
    up = jnp.concatenate([zt[-rot:], zt[:-rot]], axis=0)
    tile8_rows = lambda t: jnp.concatenate([t] * HEADS, axis=0)
    qmt = (zt * tile8_rows(cos.T) + down * tile8_rows(sin_lo.T) + up * tile8_rows(sin_hi.T)) * m_scale
    qmt_ref[0] = qmt.astype(BF16)
    kvn = _rms(zr[:, R_KVL:R_KVL + KV_LORA], rows_ref[ROW_KVG:ROW_KVG + 1, 0:KV_LORA]).astype(BF16)
    krope = rope(misc * rows_ref[ROW_ROPE:ROW_ROPE + 1, 0:LANES], cos, sin_lo, sin_hi)
    km_ref[0] = (_dot(kvn, wkk_ref[...]) + tile8(krope)).astype(BF16)
    store_value_blocks(vmt_ref, _dot_nt(wkvt_ref[...], kvn))
    store_value_blocks(vft_ref, _dot_nt(wvt_ref[...], hb))
    sgf_ref[0] = _silu(zr[:, R_FG:R_FG + FOX_W]).astype(BF16)
    sgm_ref[0] = _silu(zr[:, R_MG:R_MG + FOX_W]).astype(BF16)

    chi, cmid, clo = _split3(cum * LOG2E)
    lane = lax.broadcasted_iota(jnp.int32, (ts, LANES), 1)
    cpieces = jnp.where(
        lane < HEADS, chi.astype(F32),
        jnp.where(lane < 2 * HEADS, pltpu.roll(cmid.astype(F32), HEADS, 1),
                  jnp.where(lane < 3 * HEADS, pltpu.roll(clo.astype(F32), 2 * HEADS, 1), 0.0)))
    cb = cpieces.astype(BF16)
    aug_k = _dot(cb, ek_ref[...]) + rows_ref[ROW_KONES:ROW_KONES + 1, :]
    for hd in range(HEADS):
        pair = slice((hd // 2) * LANES, (hd // 2 + 1) * LANES)
        blk = slice(hd * LANES, (hd + 1) * LANES)
        own = (lane < HEAD_DIM) if hd % 2 == 0 else (lane >= HEAD_DIM)
        kf_ref[0, :, blk] = jnp.where(own, zk[:, pair], aug_k[:, blk]).astype(BF16)
    ones_cols = jnp.concatenate([qones_ref[...]] * (ts // LANES), axis=1)
    aug_qt = _dot_nt(eqt_ref[...], cb) + ones_cols
    for hd in range(HEADS):
        data_rows = zqt[hd * HEAD_DIM:(hd + 1) * HEAD_DIM, :].astype(BF16)
        first = slice(hd * LANES, hd * LANES + HEAD_DIM)
        second = slice(hd * LANES + HEAD_DIM, (hd + 1) * LANES)
        if hd % 2 == 0:
            qft_ref[0, first, :] = data_rows
            qft_ref[0, second, :] = aug_qt[second, :].astype(BF16)
        else:
            qft_ref[0, first, :] = aug_qt[first, :].astype(BF16)
            qft_ref[0, second, :] = data_rows


def _resident(shape):
    nd = len(shape)
    return pl.BlockSpec(shape, lambda b, s: (0,) * nd, pipeline_mode=pl.Buffered(1))


def _layer_slab(stacked, layer):
    nd = stacked.ndim - 1
    return pl.BlockSpec((None,) + stacked.shape[1:], lambda b, s: (layer,) + (0,) * nd,
                        pipeline_mode=pl.Buffered(1))


def _mod_spec(layer):
    return pl.BlockSpec((None, 1, 1, 3 * D_MODEL), lambda b, s: (layer, b, 0, 0))


def _project(x, prev, layer, mod4, norm_g3, w, consts, tables):
    batch, seq, _ = x.shape
    ts = SEQ_TILE
    fused = prev is not None
    row_tile = lambda width: pl.BlockSpec((1, ts, width), lambda b, s: (b, s, 0))
    wide = jax.ShapeDtypeStruct((batch, seq, HEAD_BLOCKS), BF16)
    narrow = jax.ShapeDtypeStruct((batch, seq, FOX_W), BF16)
    wide_t = jax.ShapeDtypeStruct((batch, seq // ts, HEADS * V_ROWS, ts), BF16)
    vt_tile = pl.BlockSpec((1, 1, HEADS * V_ROWS, ts), lambda b, s: (b, s, 0, 0))
    wide_q = jax.ShapeDtypeStruct((batch, HEAD_BLOCKS, seq), BF16)
    qt_tile = pl.BlockSpec((1, HEAD_BLOCKS, ts), lambda b, s: (b, 0, s))
    layer_weights = [w[name] for name in ("wqt", "wk", "wvt", "wr", "wuqt", "wkk", "wkvt", "rows")]

    args, in_specs = [x], [row_tile(D_MODEL)]
    if fused:
        args += [prev[0], prev[1], w["wout"], mod4]
        in_specs += [row_tile(FOX_W), row_tile(FOX_W), _layer_slab(w["wout"], layer - 1),
                     _mod_spec(layer - 1)]
    args += [mod4, norm_g3] + layer_weights + list(consts) + list(tables)
    in_specs += ([_mod_spec(layer), _layer_slab(norm_g3, layer)]
                 + [_layer_slab(a, layer) for a in layer_weights]
                 + [_resident(a.shape) for a in consts]
                 + [row_tile(LANES)] * 3)
    out_specs = [qt_tile, row_tile(HEAD_BLOCKS), vt_tile, row_tile(FOX_W)] * 2
    out_shape = [wide_q, wide, wide_t, narrow] * 2
    if fused:
        out_specs = [row_tile(D_MODEL)] + out_specs
        out_shape = [jax.ShapeDtypeStruct(x.shape, F32)] + out_shape
    outs = pl.pallas_call(
        functools.partial(_proj_kernel, fused=fused),
        grid=(batch, seq // ts),
        in_specs=in_specs, out_specs=out_specs, out_shape=out_shape,
        scratch_shapes=[pltpu.VMEM((8, LANES), F32)],
        compiler_params=_params(),
        name="out_in_proj" if fused else "in_proj",
    )(*args)
    return (outs[0], outs[1:]) if fused else (x, outs)


def _attn_kernel(qt_ref, k_ref, vt_ref, sg_ref, o_ref, m_ref, acc_ref, ahead_ref, *, chunk):
    tq = qt_ref.shape[2]
    i = pl.program_id(1)
    m_ref[...] = jnp.full(m_ref.shape, NEG_BIG, F32)
    acc_ref[...] = jnp.zeros(acc_ref.shape, F32)

    half = tq // 2
    units = [(h, part) for h in range(HEADS) for part in range(2)]

    def diag_mask(nkeys, part):
        key = lax.broadcasted_iota(jnp.int32, (nkeys, half), 0)
        qry = lax.broadcasted_iota(jnp.int32, (nkeys, half), 1) + part * half
        if chunk == 1:
            return key <= qry
        return (key // chunk) <= (qry // chunk)

    def scores(j, u, nkeys):
        h, part = units[u]
        start = pl.multiple_of(j * tq, tq)
        blk = slice(h * LANES, (h + 1) * LANES)
        return _dot(k_ref[0, pl.ds(start, nkeys), blk],
                    qt_ref[0, blk, part * half:(part + 1) * half])

    ring = ahead_ref.shape[0]
    assert len(units) % ring == 0 and ring > SCORE_LOOKAHEAD

    def kv_step(j, masked):
        def unit_keys(u):
            return half if (masked and units[u][1] == 0 and u >= SCORE_LOOKAHEAD) else tq

        for u, (h, part) in enumerate(units):
            cols = slice(part * half, (part + 1) * half)
            nkeys = unit_keys(u)
            ahead = u + SCORE_LOOKAHEAD
            if ahead < len(units):
                ahead_ref[ahead % ring, 0:unit_keys(ahead), :] = scores(j, ahead, unit_keys(ahead))
            elif not masked:
                ahead_ref[ahead % ring] = scores(j + 1, ahead - len(units), tq)

            st = ahead_ref[u % ring, 0:nkeys, :]
            if masked:
                st = jnp.where(diag_mask(nkeys, part), st, NEG_BIG)
            m_prev = m_ref[h, :, cols]
            m_new = jnp.maximum(m_prev, jnp.max(st, axis=0, keepdims=True))
            alpha = jnp.exp2(m_prev - m_new)
            p = jnp.exp2(st - jnp.tile(m_new, (nkeys // 8, 1)))
            vt = vt_ref[0, j, h * V_ROWS:(h + 1) * V_ROWS, 0:nkeys]
            acc_ref[h, :, cols] = (acc_ref[h, :, cols] * jnp.tile(alpha, (V_ROWS // 8, 1))
                                   + _dot(vt, p.astype(BF16)))
            m_ref[h, :, cols] = m_new

    for s in range(SCORE_LOOKAHEAD):
        ahead_ref[s] = scores(0, s, tq)

    @pl.loop(0, i // 2)
    def _(jj):
        kv_step(2 * jj, False)
        kv_step(2 * jj + 1, False)

    @pl.when(i % 2 == 1)
    def _():
        kv_step(i - 1, False)

    kv_step(i, True)

    for pair in range(HEADS // 2):
        halves = []
        for h in (2 * pair, 2 * pair + 1):
            a = acc_ref[h]
            halves.append(a[0:HEAD_DIM, :] / a[HEAD_DIM:HEAD_DIM + 1, :])
        out = jnp.concatenate(halves, axis=0).T
        sg = sg_ref[0, :, pair * LANES:(pair + 1) * LANES].astype(F32)
        o_ref[0, :, pair * LANES:(pair + 1) * LANES] = (out * sg).astype(o_ref.dtype)


def _attention(qt, k, vt, sg, chunk):
    batch, seq, _ = k.shape
    tq = Q_TILE
    assert qt.shape == (batch, HEAD_BLOCKS, seq)
    assert vt.shape == (batch, seq // tq, HEADS * V_ROWS, tq)
    return pl.pallas_call(
        functools.partial(_attn_kernel, chunk=chunk),
        grid=(batch, seq // tq),
        in_specs=[
            pl.BlockSpec((1, HEAD_BLOCKS, tq), lambda b, i: (b, 0, i)),
            pl.BlockSpec((1, seq, HEAD_BLOCKS), lambda b, i: (b, 0, 0)),
            pl.BlockSpec((1, seq // tq, HEADS * V_ROWS, tq), lambda b, i: (b, 0, 0, 0)),
            pl.BlockSpec((1, tq, FOX_W), lambda b, i: (b, i, 0)),
        ],
        out_specs=pl.BlockSpec((1, tq, FOX_W), lambda b, i: (b, i, 0)),
        out_shape=jax.ShapeDtypeStruct((batch, seq, FOX_W), BF16),
        scratch_shapes=[pltpu.VMEM((HEADS, 8, tq), F32), pltpu.VMEM((HEADS, V_ROWS, tq), F32),
                        pltpu.VMEM((SCORE_RING, tq, tq // 2), F32)],
        compiler_params=_params(),
        name="attn_fox" if chunk == 1 else "attn_mla",
    )(qt, k, vt, sg)


def _final_kernel(x_ref, yf_ref, ym_ref, w_ref, mod_ref, fg_ref, o_ref):
    y = _dot(yf_ref[0], w_ref[0:FOX_W, :]) + _dot(ym_ref[0], w_ref[FOX_W:2 * FOX_W, :])
    gate = mod_ref[0, :, 2 * D_MODEL:3 * D_MODEL]
    o_ref[0] = _rms(x_ref[0] + gate * y, fg_ref[...])


def _final_project(x, yf, ym, wout, mod4, final_g):
    batch, seq, _ = x.shape
    ts = SEQ_TILE
    row_tile = lambda width: pl.BlockSpec((1, ts, width), lambda b, s: (b, s, 0))
    return pl.pallas_call(
        _final_kernel,
        grid=(batch, seq // ts),
        in_specs=[row_tile(D_MODEL), row_tile(FOX_W), row_tile(FOX_W),
                  _layer_slab(wout, DEPTH - 1), _mod_spec(DEPTH - 1), _resident((1, D_MODEL))],
        out_specs=row_tile(D_MODEL),
        out_shape=jax.ShapeDtypeStruct(x.shape, F32),
        compiler_params=_params(),
        name="out_proj_final",
    )(x, yf, ym, wout, mod4, final_g.reshape(1, D_MODEL))


def kernel(x, c, positions, norm_g, w_ada, b_ada, w_in, b_f, q_norm_g, w_uq, kv_norm_g, w_ukv,
           w_out, final_g):
    batch = x.shape[0]
    mod4 = _adaln(c, w_ada, b_ada).reshape(DEPTH, batch, 1, 3 * D_MODEL)
    tables = _rope_tables(positions)
    eq, ek = _aug_matrices()
    ltri = np.tril(np.ones((SEQ_TILE, SEQ_TILE), np.float32))
    consts = (jnp.asarray(eq.T, BF16), jnp.asarray(ek, BF16), jnp.asarray(_q_ones_rows()),
              jnp.asarray(ltri, BF16))
    w = _prep_weights(w_in, w_uq, w_ukv, w_out, b_f, q_norm_g, kv_norm_g)
    norm_g3 = norm_g.reshape(DEPTH, 1, D_MODEL)
    prev = None
    for layer in range(DEPTH):
        x, (qf, kf, vf, sgf, qm, km, vm, sgm) = _project(
            x, prev, layer, mod4, norm_g3, w, consts, tables)
        prev = (_attention(qf, kf, vf, sgf, chunk=1), _attention(qm, km, vm, sgm, chunk=CHUNK))
    return _final_project(x, prev[0], prev[1], w["wout"], mod4, final_g)
```

```python
import functools

import numpy as np
import jax
import jax.numpy as jnp
from jax import lax
from jax.experimental import pallas as pl
from jax.experimental.pallas import tpu as pltpu

D_MODEL = 1024
DEPTH = 4
CHUNK = 64
HEADS = 8
HEAD_DIM = 64
FOX_W = HEADS * HEAD_DIM
MLA_ROPE = 32
MLA_QK = HEAD_DIM + MLA_ROPE
Q_LORA = 256
KV_LORA = 128
ROPE_THETA = 10000.0
EPS = 1e-6

LANES = 128
HEAD_BLOCKS = HEADS * LANES
LOG2E = 1.4426950408889634
NEG_BIG = -1e30

SEQ_TILE = 512
Q_TILE = 512
SUB_TILES = 2
SCORE_LOOKAHEAD = 3
SCORE_RING = 4
V_ROWS = 80
VMEM_LIMIT = 56 * 1024 * 1024


R_MISC, R_FG, R_QL, R_KVL, R_MG, R_END = 0, 128, 640, 896, 1024, 1536

ROW_KONES, ROW_BF, ROW_QG, ROW_KVG, ROW_ROPE = 1, 3, 4, 5, 6

F32 = jnp.float32
BF16 = jnp.bfloat16


def _dot(a, b):
    return jnp.dot(a, b, preferred_element_type=F32)


def _dot_nt(a, b):
    return lax.dot_general(a, b, (((1,), (1,)), ((), ())), preferred_element_type=F32)


def _split3(v):
    hi = v.astype(BF16)
    r1 = v - hi.astype(F32)
    mid = r1.astype(BF16)
    lo = (r1 - mid.astype(F32)).astype(BF16)
    return hi, mid, lo


def _rms(v, g):
    return v * lax.rsqrt(jnp.mean(v * v, axis=-1, keepdims=True) + EPS) * g


def _silu(v):
    return v * jax.nn.sigmoid(v)


def _params():
    return pltpu.CompilerParams(
        dimension_semantics=("arbitrary", "arbitrary"), vmem_limit_bytes=VMEM_LIMIT)


def _adaln_kernel(c_ref, w_ref, b_ref, o_ref):
    cv = c_ref[...]
    ca = (cv * jax.nn.sigmoid(cv)).astype(BF16)
    o_ref[0] = _dot(ca, w_ref[0].astype(BF16)) + b_ref[0]


def _adaln(c, w_ada, b_ada):
    batch = c.shape[0]
    n_tiles = 3
    return pl.pallas_call(
        _adaln_kernel,
        grid=(DEPTH, n_tiles),
        in_specs=[
            pl.BlockSpec((batch, D_MODEL), lambda l, n: (0, 0)),
            pl.BlockSpec((1, D_MODEL, D_MODEL), lambda l, n: (l, 0, n)),
            pl.BlockSpec((1, 1, D_MODEL), lambda l, n: (l, 0, n)),
        ],
        out_specs=pl.BlockSpec((1, batch, D_MODEL), lambda l, n: (l, 0, n)),
        out_shape=jax.ShapeDtypeStruct((DEPTH, batch, 3 * D_MODEL), F32),
        compiler_params=_params(),
        name="adaln_mod",
    )(c, w_ada, b_ada.reshape(DEPTH, 1, 3 * D_MODEL))


def _rope_table_kernel(pos_ref, invf_ref, cos_ref, sin_lo_ref, sin_hi_ref):
    ang = pos_ref[0].astype(F32) * invf_ref[...]
    sin = jnp.sin(ang)
    lane = lax.broadcasted_iota(jnp.int32, ang.shape, 1)
    first = lane < HEAD_DIM + MLA_ROPE // 2
    cos_ref[0] = jnp.cos(ang)
    sin_lo_ref[0] = jnp.where(first, -sin, 0.0)
    sin_hi_ref[0] = jnp.where(first, 0.0, sin)


def _rope_tables(positions):
    batch, seq = positions.shape
    inv_freq = 1.0 / (ROPE_THETA ** (jnp.arange(0, MLA_ROPE, 2, dtype=F32) / MLA_ROPE))
    half = MLA_ROPE // 2
    invf_row = jnp.zeros((1, LANES), F32)
    invf_row = invf_row.at[0, HEAD_DIM:HEAD_DIM + half].set(inv_freq)
    invf_row = invf_row.at[0, HEAD_DIM + half:HEAD_DIM + MLA_ROPE].set(inv_freq)
    ts = SEQ_TILE
    out = jax.ShapeDtypeStruct((batch, seq, LANES), F32)
    return pl.pallas_call(
        _rope_table_kernel,
        grid=(batch, seq // ts),
        in_specs=[
            pl.BlockSpec((1, ts, 1), lambda b, s: (b, s, 0)),
            pl.BlockSpec((1, LANES), lambda b, s: (0, 0)),
        ],
        out_specs=[pl.BlockSpec((1, ts, LANES), lambda b, s: (b, s, 0))] * 3,
        out_shape=[out, out, out],
        compiler_params=_params(),
        name="rope_tables",
    )(positions.reshape(batch, seq, 1), invf_row)


def _fox_aug_lane(h):
    return h * LANES + (HEAD_DIM if h % 2 == 0 else 0)


def _const_rows():
    rows = np.zeros((8, HEAD_BLOCKS), np.float32)
    for h in range(HEADS):
        base = _fox_aug_lane(h)
        rows[ROW_KONES, base:base + 3] = 1.0
    rows[ROW_ROPE, HEAD_DIM:MLA_QK] = 1.0
    return rows


def _q_ones_rows():
    col = np.zeros((HEAD_BLOCKS, LANES), np.float32)
    for h in range(HEADS):
        base = _fox_aug_lane(h)
        col[base + 3:base + 6, :] = 1.0
    return col


def _aug_matrices():
    eq = np.zeros((LANES, HEAD_BLOCKS), np.float32)
    ek = np.zeros((LANES, HEAD_BLOCKS), np.float32)
    for p in range(3):
        for h in range(HEADS):
            eq[p * HEADS + h, _fox_aug_lane(h) + p] = 1.0
            ek[p * HEADS + h, _fox_aug_lane(h) + 3 + p] = -1.0
    return eq, ek


def _prep_weights(w_in, w_uq, w_ukv, w_out, b_f, q_norm_g, kv_norm_g):
    o = np.cumsum([0, FOX_W, FOX_W, FOX_W, HEADS, FOX_W, Q_LORA, KV_LORA, MLA_ROPE, FOX_W])
    w_fq, w_fk, w_fv, w_ff, w_fg, w_ql, w_kvl, w_kr, w_mg = (
        w_in[:, :, o[i]:o[i + 1]] for i in range(9))
    misc = jnp.zeros((DEPTH, D_MODEL, LANES), F32)
    misc = misc.at[:, :, 0:HEADS].set(w_ff).at[:, :, HEAD_DIM:MLA_QK].set(w_kr)
    wr = jnp.concatenate([misc, w_fg, w_ql, w_kvl, w_mg], axis=2).astype(BF16)

    uq = w_uq.reshape(DEPTH, Q_LORA, HEADS, MLA_QK)
    wuq = jnp.pad(uq, ((0, 0), (0, 0), (0, 0), (0, LANES - MLA_QK)))
    ukv = w_ukv.reshape(DEPTH, KV_LORA, HEADS, 2 * HEAD_DIM)
    wkk = jnp.pad(ukv[..., :HEAD_DIM], ((0, 0), (0, 0), (0, 0), (0, LANES - HEAD_DIM)))
    wkvt = jnp.swapaxes(ukv[..., HEAD_DIM:].reshape(DEPTH, KV_LORA, FOX_W), 1, 2)

    rows = jnp.broadcast_to(jnp.asarray(_const_rows()), (DEPTH, 8, HEAD_BLOCKS))
    rows = rows.at[:, ROW_BF, :HEADS].set(b_f)
    rows = rows.at[:, ROW_QG, :Q_LORA].set(q_norm_g)
    rows = rows.at[:, ROW_KVG, :KV_LORA].set(kv_norm_g)
    return dict(
        wqt=jnp.swapaxes(w_fq, 1, 2).astype(BF16),
        wk=w_fk.astype(BF16),
        wvt=jnp.swapaxes(w_fv, 1, 2).astype(BF16),
        wr=wr,
        wuqt=jnp.swapaxes(wuq.reshape(DEPTH, Q_LORA, HEAD_BLOCKS), 1, 2).astype(BF16),
        wkk=wkk.reshape(DEPTH, KV_LORA, HEAD_BLOCKS).astype(BF16),
        wkvt=wkvt.astype(BF16),
        rows=rows, wout=w_out.astype(BF16))


def _proj_kernel(*refs, fused):
    if fused:
        x_ref, yf_ref, ym_ref, wout_ref, modp_ref, *refs = refs
    else:
        x_ref, *refs = refs
    (mod_ref, g_ref, wqt_ref, wk_ref, wvt_ref, wr_ref, wuqt_ref, wkk_ref, wkvt_ref, rows_ref,
     eqt_ref, ek_ref, qones_ref, ltri_ref, cos_ref, sin_lo_ref, sin_hi_ref, *outs) = refs
    if fused:
        xo_ref, *outs = outs
    qft_ref, kf_ref, vft_ref, sgf_ref, qmt_ref, km_ref, vmt_ref, sgm_ref, carry_ref = outs

    ts = x_ref.shape[1]
    pad_rows = lax.broadcasted_iota(jnp.int32, (V_ROWS - HEAD_DIM, ts), 0)
    ones_then_zeros = jnp.where(pad_rows == 0, 1.0, 0.0).astype(BF16)

    def store_value_blocks(out_ref, vt):
        for h in range(HEADS):
            out_ref[0, 0, h * V_ROWS:h * V_ROWS + HEAD_DIM, :] = (
                vt[h * HEAD_DIM:(h + 1) * HEAD_DIM, :].astype(BF16))
            out_ref[0, 0, h * V_ROWS + HEAD_DIM:(h + 1) * V_ROWS, :] = ones_then_zeros

    def rope(t, cos, sin_lo, sin_hi):
        width = t.shape[1]
        down = pltpu.roll(t, width - MLA_ROPE // 2, 1)
        up = pltpu.roll(t, MLA_ROPE // 2, 1)
        return t * cos + down * sin_lo + up * sin_hi

    @pl.when(pl.program_id(1) == 0)
    def _():
        carry_ref[...] = jnp.zeros_like(carry_ref)

    xs = x_ref[0]
    if fused:
        y = (_dot(yf_ref[0], wout_ref[0:FOX_W, :]) + _dot(ym_ref[0], wout_ref[FOX_W:2 * FOX_W, :]))
        xs = xs + modp_ref[0, :, 2 * D_MODEL:3 * D_MODEL] * y
        xo_ref[0] = xs
    shift = mod_ref[0, :, 0:D_MODEL]
    scale = mod_ref[0, :, D_MODEL:2 * D_MODEL]
    h = _rms(xs, g_ref[...]) * (1.0 + scale) + shift
    hb = h.astype(BF16)

    zr = _dot(hb, wr_ref[...])
    misc = zr[:, R_MISC:R_MISC + LANES]
    q_scale = HEAD_DIM ** -0.5 * LOG2E
    zqt = _dot_nt(wqt_ref[...], hb) * q_scale

    ff = misc + rows_ref[ROW_BF:ROW_BF + 1, 0:LANES]
    lf = jnp.minimum(ff, 0.0) - jnp.log1p(jnp.exp(-jnp.abs(ff)))
    hi, mid, lo = _split3(lf)
    ltri = ltri_ref[...]
    cum = _dot(ltri, hi) + _dot(ltri, mid) + _dot(ltri, lo) + carry_ref[0:1, :]
    carry_ref[...] = jnp.broadcast_to(cum[ts - 1:ts, :], carry_ref.shape)

    zk = _dot(hb, wk_ref[...])

    cos, sin_lo, sin_hi = cos_ref[0], sin_lo_ref[0], sin_hi_ref[0]
    tile8 = lambda t: jnp.concatenate([t] * HEADS, axis=1)
    qn = _rms(zr[:, R_QL:R_QL + Q_LORA], rows_ref[ROW_QG:ROW_QG + 1, 0:Q_LORA]).astype(BF16)
    m_scale = MLA_QK ** -0.5 * LOG2E
    zt = _dot_nt(wuqt_ref[...], qn)
    rot = MLA_ROPE // 2
    down = jnp.concatenate([zt[rot:], zt[:rot]], axis=0)
    up = jnp.concatenate([zt[-rot:], zt[:-rot]], axis=0)
    tile8_rows = lambda t: jnp.concatenate([t] * HEADS, axis=0)
    qmt = (zt * tile8_rows(cos.T) + down * tile8_rows(sin_lo.T) + up * tile8_rows(sin_hi.T)) * m_scale
    qmt_ref[0] = qmt.astype(BF16)
    kvn = _rms(zr[:, R_KVL:R_KVL + KV_LORA], rows_ref[ROW_KVG:ROW_KVG + 1, 0:KV_LORA]).astype(BF16)
    krope = rope(misc * rows_ref[ROW_ROPE:ROW_ROPE + 1, 0:LANES], cos, sin_lo, sin_hi)
    km_ref[0] = (_dot(kvn, wkk_ref[...]) + tile8(krope)).astype(BF16)
    store_value_blocks(vmt_ref, _dot_nt(wkvt_ref[...], kvn))
    store_value_blocks(vft_ref, _dot_nt(wvt_ref[...], hb))
    sgf_ref[0] = _silu(zr[:, R_FG:R_FG + FOX_W]).astype(BF16)
    sgm_ref[0] = _silu(zr[:, R_MG:R_MG + FOX_W]).astype(BF16)

    chi, cmid, clo = _split3(cum * LOG2E)
    lane = lax.broadcasted_iota(jnp.int32, (ts, LANES), 1)
    cpieces = jnp.where(
        lane < HEADS, chi.astype(F32),
        jnp.where(lane < 2 * HEADS, pltpu.roll(cmid.astype(F32), HEADS, 1),
                  jnp.where(lane < 3 * HEADS, pltpu.roll(clo.astype(F32), 2 * HEADS, 1), 0.0)))
    cb = cpieces.astype(BF16)
    aug_k = _dot(cb, ek_ref[...]) + rows_ref[ROW_KONES:ROW_KONES + 1, :]
    for hd in range(HEADS):
        pair = slice((hd // 2) * LANES, (hd // 2 + 1) * LANES)
        blk = slice(hd * LANES, (hd + 1) * LANES)
        own = (lane < HEAD_DIM) if hd % 2 == 0 else (lane >= HEAD_DIM)
        kf_ref[0, :, blk] = jnp.where(own, zk[:, pair], aug_k[:, blk]).astype(BF16)
    ones_cols = jnp.concatenate([qones_ref[...]] * (ts // LANES), axis=1)
    aug_qt = _dot_nt(eqt_ref[...], cb) + ones_cols
    for hd in range(HEADS):
        data_rows = zqt[hd * HEAD_DIM:(hd + 1) * HEAD_DIM, :].astype(BF16)
        first = slice(hd * LANES, hd * LANES + HEAD_DIM)
        second = slice(hd * LANES + HEAD_DIM, (hd + 1) * LANES)
        if hd % 2 == 0:
            qft_ref[0, first, :] = data_rows
            qft_ref[0, second, :] = aug_qt[second, :].astype(BF16)
        else:
            qft_ref[0, first, :] = aug_qt[first, :].astype(BF16)
            qft_ref[0, second, :] = data_rows


def _resident(shape):
    nd = len(shape)
    return pl.BlockSpec(shape, lambda b, s: (0,) * nd, pipeline_mode=pl.Buffered(1))


def _layer_slab(stacked, layer):
    nd = stacked.ndim - 1
    return pl.BlockSpec((None,) + stacked.shape[1:], lambda b, s: (layer,) + (0,) * nd,
                        pipeline_mode=pl.Buffered(1))


def _mod_spec(layer):
    return pl.BlockSpec((None, 1, 1, 3 * D_MODEL), lambda b, s: (layer, b, 0, 0))


def _project(x, prev, layer, mod4, norm_g3, w, consts, tables):
    batch, seq, _ = x.shape
    ts = SEQ_TILE
    fused = prev is not None
    row_tile = lambda width: pl.BlockSpec((1, ts, width), lambda b, s: (b, s, 0))
    wide = jax.ShapeDtypeStruct((batch, seq, HEAD_BLOCKS), BF16)
    narrow = jax.ShapeDtypeStruct((batch, seq, FOX_W), BF16)
    wide_t = jax.ShapeDtypeStruct((batch, seq // ts, HEADS * V_ROWS, ts), BF16)
    vt_tile = pl.BlockSpec((1, 1, HEADS * V_ROWS, ts), lambda b, s: (b, s, 0, 0))
    wide_q = jax.ShapeDtypeStruct((batch, HEAD_BLOCKS, seq), BF16)
    qt_tile = pl.BlockSpec((1, HEAD_BLOCKS, ts), lambda b, s: (b, 0, s))
    layer_weights = [w[name] for name in ("wqt", "wk", "wvt", "wr", "wuqt", "wkk", "wkvt", "rows")]

    args, in_specs = [x], [row_tile(D_MODEL)]
    if fused:
        args += [prev[0], prev[1], w["wout"], mod4]
        in_specs += [row_tile(FOX_W), row_tile(FOX_W), _layer_slab(w["wout"], layer - 1),
                     _mod_spec(layer - 1)]
    args += [mod4, norm_g3] + layer_weights + list(consts) + list(tables)
    in_specs += ([_mod_spec(layer), _layer_slab(norm_g3, layer)]
                 + [_layer_slab(a, layer) for a in layer_weights]
                 + [_resident(a.shape) for a in consts]
                 + [row_tile(LANES)] * 3)
    out_specs = [qt_tile, row_tile(HEAD_BLOCKS), vt_tile, row_tile(FOX_W)] * 2
    out_shape = [wide_q, wide, wide_t, narrow] * 2
    if fused:
        out_specs = [row_tile(D_MODEL)] + out_specs
        out_shape = [jax.ShapeDtypeStruct(x.shape, F32)] + out_shape
    outs = pl.pallas_call(
        functools.partial(_proj_kernel, fused=fused),
        grid=(batch, seq // ts),
        in_specs=in_specs, out_specs=out_specs, out_shape=out_shape,
        scratch_shapes=[pltpu.VMEM((8, LANES), F32)],
        compiler_params=_params(),
        name="out_in_proj" if fused else "in_proj",
    )(*args)
    return (outs[0], outs[1:]) if fused else (x, outs)


def _attn_kernel(qt_ref, k_ref, vt_ref, sg_ref, o_ref, m_ref, acc_ref, ahead_ref, *, chunk):
    for sub in range(SUB_TILES):
        _attn_sweep(SUB_TILES * pl.program_id(1) + sub, sub * Q_TILE, qt_ref, k_ref, vt_ref, sg_ref,
                    o_ref, m_ref.at[sub], acc_ref.at[sub], ahead_ref.at[sub], chunk=chunk)


def _attn_sweep(i, off, qt_ref, k_ref, vt_ref, sg_ref, o_ref, m_ref, acc_ref, ahead_ref, *, chunk):
    tq = Q_TILE
    m_ref[...] = jnp.full(m_ref.shape, NEG_BIG, F32)
    acc_ref[...] = jnp.zeros(acc_ref.shape, F32)

    half = tq // 2
    units = [(h, part) for h in range(HEADS) for part in range(2)]

    def diag_mask(nkeys, part):
        key = lax.broadcasted_iota(jnp.int32, (nkeys, half), 0)
        qry = lax.broadcasted_iota(jnp.int32, (nkeys, half), 1) + part * half
        if chunk == 1:
            return key <= qry
        return (key // chunk) <= (qry // chunk)

    def scores(j, u, nkeys):
        h, part = units[u]
        start = pl.multiple_of(j * tq, tq)
        blk = slice(h * LANES, (h + 1) * LANES)
        return _dot(k_ref[0, pl.ds(start, nkeys), blk],
                    qt_ref[0, blk, off + part * half:off + (part + 1) * half])

    ring = ahead_ref.shape[0]
    assert len(units) % ring == 0 and ring > SCORE_LOOKAHEAD

    def kv_step(j, masked):
        def unit_keys(u):
            return half if (masked and units[u][1] == 0 and u >= SCORE_LOOKAHEAD) else tq

        for u, (h, part) in enumerate(units):
            cols = slice(part * half, (part + 1) * half)
            nkeys = unit_keys(u)
            ahead = u + SCORE_LOOKAHEAD
            if ahead < len(units):
                ahead_ref[ahead % ring, 0:unit_keys(ahead), :] = scores(j, ahead, unit_keys(ahead))
            elif not masked:
                ahead_ref[ahead % ring] = scores(j + 1, ahead - len(units), tq)

            st = ahead_ref[u % ring, 0:nkeys, :]
            if masked:
                st = jnp.where(diag_mask(nkeys, part), st, NEG_BIG)
            m_prev = m_ref[h, :, cols]
            m_new = jnp.maximum(m_prev, jnp.max(st, axis=0, keepdims=True))
            alpha = jnp.exp2(m_prev - m_new)
            p = jnp.exp2(st - jnp.tile(m_new, (nkeys // 8, 1)))
            vt = vt_ref[0, j, h * V_ROWS:(h + 1) * V_ROWS, 0:nkeys]
            acc_ref[h, :, cols] = (acc_ref[h, :, cols] * jnp.tile(alpha, (V_ROWS // 8, 1))
                                   + _dot(vt, p.astype(BF16)))
            m_ref[h, :, cols] = m_new

    for s in range(SCORE_LOOKAHEAD):
        ahead_ref[s] = scores(0, s, tq)

    @pl.loop(0, i // 2)
    def _(jj):
        kv_step(2 * jj, False)
        kv_step(2 * jj + 1, False)

    @pl.when(i % 2 == 1)
    def _():
        kv_step(i - 1, False)

    kv_step(i, True)

    for pair in range(HEADS // 2):
        halves = []
        for h in (2 * pair, 2 * pair + 1):
            a = acc_ref[h]
            halves.append(a[0:HEAD_DIM, :] / a[HEAD_DIM:HEAD_DIM + 1, :])
        out = jnp.concatenate(halves, axis=0).T
        sg = sg_ref[0, off:off + tq, pair * LANES:(pair + 1) * LANES].astype(F32)
        o_ref[0, off:off + tq, pair * LANES:(pair + 1) * LANES] = (out * sg).astype(o_ref.dtype)


def _attention(qt, k, vt, sg, chunk):
    batch, seq, _ = k.shape
    tq = Q_TILE
    assert qt.shape == (batch, HEAD_BLOCKS, seq)
    assert vt.shape == (batch, seq // tq, HEADS * V_ROWS, tq)
    step = SUB_TILES * tq
    return pl.pallas_call(
        functools.partial(_attn_kernel, chunk=chunk),
        grid=(batch, seq // step),
        in_specs=[
            pl.BlockSpec((1, HEAD_BLOCKS, step), lambda b, i: (b, 0, i)),
            pl.BlockSpec((1, seq, HEAD_BLOCKS), lambda b, i: (b, 0, 0)),
            pl.BlockSpec((1, seq // tq, HEADS * V_ROWS, tq), lambda b, i: (b, 0, 0, 0)),
            pl.BlockSpec((1, step, FOX_W), lambda b, i: (b, i, 0)),
        ],
        out_specs=pl.BlockSpec((1, step, FOX_W), lambda b, i: (b, i, 0)),
        out_shape=jax.ShapeDtypeStruct((batch, seq, FOX_W), BF16),
        scratch_shapes=[pltpu.VMEM((SUB_TILES, HEADS, 8, tq), F32),
                        pltpu.VMEM((SUB_TILES, HEADS, V_ROWS, tq), F32),
                        pltpu.VMEM((SUB_TILES, SCORE_RING, tq, tq // 2), F32)],
        compiler_params=_params(),
        name="attn_fox" if chunk == 1 else "attn_mla",
    )(qt, k, vt, sg)


def _final_kernel(x_ref, yf_ref, ym_ref, w_ref, mod_ref, fg_ref, o_ref):
    y = _dot(yf_ref[0], w_ref[0:FOX_W, :]) + _dot(ym_ref[0], w_ref[FOX_W:2 * FOX_W, :])
    gate = mod_ref[0, :, 2 * D_MODEL:3 * D_MODEL]
    o_ref[0] = _rms(x_ref[0] + gate * y, fg_ref[...])


def _final_project(x, yf, ym, wout, mod4, final_g):
    batch, seq, _ = x.shape
    ts = SEQ_TILE
    row_tile = lambda width: pl.BlockSpec((1, ts, width), lambda b, s: (b, s, 0))
    return pl.pallas_call(
        _final_kernel,
        grid=(batch, seq // ts),
        in_specs=[row_tile(D_MODEL), row_tile(FOX_W), row_tile(FOX_W),
                  _layer_slab(wout, DEPTH - 1), _mod_spec(DEPTH - 1), _resident((1, D_MODEL))],
        out_specs=row_tile(D_MODEL),
        out_shape=jax.ShapeDtypeStruct(x.shape, F32),
        compiler_params=_params(),
        name="out_proj_final",
    )(x, yf, ym, wout, mod4, final_g.reshape(1, D_MODEL))


def kernel(x, c, positions, norm_g, w_ada, b_ada, w_in, b_f, q_norm_g, w_uq, kv_norm_g, w_ukv,
           w_out, final_g):
    batch = x.shape[0]
    mod4 = _adaln(c, w_ada, b_ada).reshape(DEPTH, batch, 1, 3 * D_MODEL)
    tables = _rope_tables(positions)
    eq, ek = _aug_matrices()
    ltri = np.tril(np.ones((SEQ_TILE, SEQ_TILE), np.float32))
    consts = (jnp.asarray(eq.T, BF16), jnp.asarray(ek, BF16), jnp.asarray(_q_ones_rows()),
              jnp.asarray(ltri, BF16))
    w = _prep_weights(w_in, w_uq, w_ukv, w_out, b_f, q_norm_g, kv_norm_g)
    norm_g3 = norm_g.reshape(DEPTH, 1, D_MODEL)
    prev = None
    for layer in range(DEPTH):
        x, (qf, kf, vf, sgf, qm, km, vm, sgm) = _project(
            x, prev, layer, mod4, norm_g3, w, consts, tables)
        prev = (_attention(qf, kf, vf, sgf, chunk=1), _attention(qm, km, vm, sgm, chunk=CHUNK))
    return _final_project(x, prev[0], prev[1], w["wout"], mod4, final_g)
```

```python
import functools

import numpy as np
import jax
import jax.numpy as jnp
from jax import lax
from jax.experimental import pallas as pl
from jax.experimental.pallas import tpu as pltpu

D_MODEL = 1024
DEPTH = 4
CHUNK = 64
HEADS = 8
HEAD_DIM = 64
FOX_W = HEADS * HEAD_DIM
MLA_ROPE = 32
MLA_QK = HEAD_DIM + MLA_ROPE
Q_LORA = 256
KV_LORA = 128
ROPE_THETA = 10000.0
EPS = 1e-6

LANES = 128
HEAD_BLOCKS = HEADS * LANES
LOG2E = 1.4426950408889634
NEG_BIG = -1e30

SEQ_TILE = 512
FINAL_TILE = 1024
Q_TILE = 512
SCORE_LOOKAHEAD = 4
SCORE_RING = 8
V_ROWS = 80
VMEM_LIMIT = 56 * 1024 * 1024


R_MISC, R_FG, R_QL, R_KVL, R_MG, R_END = 0, 128, 640, 896, 1024, 1536

ROW_KONES, ROW_BF, ROW_QG, ROW_KVG, ROW_ROPE = 1, 3, 4, 5, 6

F32 = jnp.float32
BF16 = jnp.bfloat16


def _dot(a, b):
    return jnp.dot(a, b, preferred_element_type=F32)


def _dot_nt(a, b):
    return lax.dot_general(a, b, (((1,), (1,)), ((), ())), preferred_element_type=F32)


def _split3(v):
    hi = v.astype(BF16)
    r1 = v - hi.astype(F32)
    mid = r1.astype(BF16)
    lo = (r1 - mid.astype(F32)).astype(BF16)
    return hi, mid, lo


def _rms(v, g):
    return v * lax.rsqrt(jnp.mean(v * v, axis=-1, keepdims=True) + EPS) * g


def _silu(v):
    return v * jax.nn.sigmoid(v)


def _params():
    return pltpu.CompilerParams(
        dimension_semantics=("arbitrary", "arbitrary"), vmem_limit_bytes=VMEM_LIMIT)


def _adaln_kernel(c_ref, w_ref, b_ref, o_ref):
    cv = c_ref[...]
    ca = (cv * jax.nn.sigmoid(cv)).astype(BF16)
    o_ref[0] = _dot(ca, w_ref[0].astype(BF16)) + b_ref[0]


def _adaln(c, w_ada, b_ada):
    batch = c.shape[0]
    n_tiles = 3
    return pl.pallas_call(
        _adaln_kernel,
        grid=(DEPTH, n_tiles),
        in_specs=[
            pl.BlockSpec((batch, D_MODEL), lambda l, n: (0, 0)),
            pl.BlockSpec((1, D_MODEL, D_MODEL), lambda l, n: (l, 0, n)),
            pl.BlockSpec((1, 1, D_MODEL), lambda l, n: (l, 0, n)),
        ],
        out_specs=pl.BlockSpec((1, batch, D_MODEL), lambda l, n: (l, 0, n)),
        out_shape=jax.ShapeDtypeStruct((DEPTH, batch, 3 * D_MODEL), F32),
        compiler_params=_params(),
        name="adaln_mod",
    )(c, w_ada, b_ada.reshape(DEPTH, 1, 3 * D_MODEL))


def _rope_table_kernel(pos_ref, invf_ref, cos_ref, sin_lo_ref, sin_hi_ref):
    ang = pos_ref[0].astype(F32) * invf_ref[...]
    sin = jnp.sin(ang)
    lane = lax.broadcasted_iota(jnp.int32, ang.shape, 1)
    first = lane < HEAD_DIM + MLA_ROPE // 2
    cos_ref[0] = jnp.cos(ang)
    sin_lo_ref[0] = jnp.where(first, -sin, 0.0)
    sin_hi_ref[0] = jnp.where(first, 0.0, sin)


def _rope_tables(positions):
    batch, seq = positions.shape
    inv_freq = 1.0 / (ROPE_THETA ** (jnp.arange(0, MLA_ROPE, 2, dtype=F32) / MLA_ROPE))
    half = MLA_ROPE // 2
    invf_row = jnp.zeros((1, LANES), F32)
    invf_row = invf_row.at[0, HEAD_DIM:HEAD_DIM + half].set(inv_freq)
    invf_row = invf_row.at[0, HEAD_DIM + half:HEAD_DIM + MLA_ROPE].set(inv_freq)
    ts = SEQ_TILE
    out = jax.ShapeDtypeStruct((batch, seq, LANES), F32)
    return pl.pallas_call(
        _rope_table_kernel,
        grid=(batch, seq // ts),
        in_specs=[
            pl.BlockSpec((1, ts, 1), lambda b, s: (b, s, 0)),
            pl.BlockSpec((1, LANES), lambda b, s: (0, 0)),
        ],
        out_specs=[pl.BlockSpec((1, ts, LANES), lambda b, s: (b, s, 0))] * 3,
        out_shape=[out, out, out],
        compiler_params=_params(),
        name="rope_tables",
    )(positions.reshape(batch, seq, 1), invf_row)


def _fox_aug_lane(h):
    return h * LANES + (HEAD_DIM if h % 2 == 0 else 0)


def _const_rows():
    rows = np.zeros((8, HEAD_BLOCKS), np.float32)
    for h in range(HEADS):
        base = _fox_aug_lane(h)
        rows[ROW_KONES, base:base + 3] = 1.0
    rows[ROW_ROPE, HEAD_DIM:MLA_QK] = 1.0
    return rows


def _q_ones_rows():
    col = np.zeros((HEAD_BLOCKS, LANES), np.float32)
    for h in range(HEADS):
        base = _fox_aug_lane(h)
        col[base + 3:base + 6, :] = 1.0
    return col


def _aug_matrices():
    eq = np.zeros((LANES, HEAD_BLOCKS), np.float32)
    ek = np.zeros((LANES, HEAD_BLOCKS), np.float32)
    for p in range(3):
        for h in range(HEADS):
            eq[p * HEADS + h, _fox_aug_lane(h) + p] = 1.0
            ek[p * HEADS + h, _fox_aug_lane(h) + 3 + p] = -1.0
    return eq, ek


def _prep_weights(w_in, w_uq, w_ukv, w_out, b_f, q_norm_g, kv_norm_g):
    o = np.cumsum([0, FOX_W, FOX_W, FOX_W, HEADS, FOX_W, Q_LORA, KV_LORA, MLA_ROPE, FOX_W])
    w_in = w_in.astype(BF16)
    w_fq, w_fk, w_fv, w_ff, w_fg, w_ql, w_kvl, w_kr, w_mg = (
        w_in[:, :, o[i]:o[i + 1]] for i in range(9))
    misc = jnp.zeros((DEPTH, D_MODEL, LANES), BF16)
    misc = misc.at[:, :, 0:HEADS].set(w_ff).at[:, :, HEAD_DIM:MLA_QK].set(w_kr)
    wr = jnp.concatenate([misc, w_fg, w_ql, w_kvl, w_mg], axis=2)

    uq = w_uq.reshape(DEPTH, Q_LORA, HEADS, MLA_QK)
    wuq = jnp.pad(uq, ((0, 0), (0, 0), (0, 0), (0, LANES - MLA_QK)))
    ukv = w_ukv.reshape(DEPTH, KV_LORA, HEADS, 2 * HEAD_DIM)
    wkk = jnp.pad(ukv[..., :HEAD_DIM], ((0, 0), (0, 0), (0, 0), (0, LANES - HEAD_DIM)))
    wkvt = jnp.swapaxes(ukv[..., HEAD_DIM:].reshape(DEPTH, KV_LORA, FOX_W), 1, 2)

    rows = jnp.broadcast_to(jnp.asarray(_const_rows()), (DEPTH, 8, HEAD_BLOCKS))
    rows = rows.at[:, ROW_BF, :HEADS].set(b_f)
    rows = rows.at[:, ROW_QG, :Q_LORA].set(q_norm_g)
    rows = rows.at[:, ROW_KVG, :KV_LORA].set(kv_norm_g)
    return dict(
        wqt=jnp.swapaxes(w_fq, 1, 2).astype(BF16),
        wk=w_fk.astype(BF16),
        wvt=jnp.swapaxes(w_fv, 1, 2).astype(BF16),
        wr=wr,
        wuqt=jnp.swapaxes(wuq.reshape(DEPTH, Q_LORA, HEAD_BLOCKS), 1, 2).astype(BF16),
        wkk=wkk.reshape(DEPTH, KV_LORA, HEAD_BLOCKS).astype(BF16),
        wkvt=wkvt.astype(BF16),
        rows=rows, wout=w_out.astype(BF16))


def _proj_kernel(*refs, fused):
    if fused:
        x_ref, yf_ref, ym_ref, wout_ref, modp_ref, *refs = refs
    else:
        x_ref, *refs = refs
    (mod_ref, g_ref, wqt_ref, wk_ref, wvt_ref, wr_ref, wuqt_ref, wkk_ref, wkvt_ref, rows_ref,
     eqt_ref, ek_ref, qones_ref, ltri_ref, cos_ref, sin_lo_ref, sin_hi_ref, *outs) = refs
    if fused:
        xo_ref, *outs = outs
    qft_ref, kf_ref, vft_ref, sgf_ref, qmt_ref, km_ref, vmt_ref, sgm_ref, carry_ref = outs

    ts = x_ref.shape[1]
    pad_rows = lax.broadcasted_iota(jnp.int32, (V_ROWS - HEAD_DIM, ts), 0)
    ones_then_zeros = jnp.where(pad_rows == 0, 1.0, 0.0).astype(BF16)

    def store_value_blocks(out_ref, vt):
        for h in range(HEADS):
            out_ref[0, 0, h * V_ROWS:h * V_ROWS + HEAD_DIM, :] = (
                vt[h * HEAD_DIM:(h + 1) * HEAD_DIM, :].astype(BF16))
            out_ref[0, 0, h * V_ROWS + HEAD_DIM:(h + 1) * V_ROWS, :] = ones_then_zeros

    def rope(t, cos, sin_lo, sin_hi):
        width = t.shape[1]
        down = pltpu.roll(t, width - MLA_ROPE // 2, 1)
        up = pltpu.roll(t, MLA_ROPE // 2, 1)
        return t * cos + down * sin_lo + up * sin_hi

    @pl.when(pl.program_id(1) == 0)
    def _():
        carry_ref[...] = jnp.zeros_like(carry_ref)

    xs = x_ref[0]
    if fused:
        y = (_dot(yf_ref[0], wout_ref[0:FOX_W, :]) + _dot(ym_ref[0], wout_ref[FOX_W:2 * FOX_W, :]))
        xs = xs + modp_ref[0, :, 2 * D_MODEL:3 * D_MODEL] * y
        xo_ref[0] = xs
    shift = mod_ref[0, :, 0:D_MODEL]
    scale = mod_ref[0, :, D_MODEL:2 * D_MODEL]
    h = _rms(xs, g_ref[...]) * (1.0 + scale) + shift
    hb = h.astype(BF16)

    zr = _dot(hb, wr_ref[...])
    misc = zr[:, R_MISC:R_MISC + LANES]
    q_scale = HEAD_DIM ** -0.5 * LOG2E
    zqt = _dot_nt(wqt_ref[...], hb) * q_scale

    ff = misc + rows_ref[ROW_BF:ROW_BF + 1, 0:LANES]
    lf = jnp.minimum(ff, 0.0) - jnp.log1p(jnp.exp(-jnp.abs(ff)))
    hi, mid, lo = _split3(lf)
    ltri = ltri_ref[...]
    cum = _dot(ltri, hi) + _dot(ltri, mid) + _dot(ltri, lo) + carry_ref[0:1, :]
    carry_ref[...] = jnp.broadcast_to(cum[ts - 1:ts, :], carry_ref.shape)

    zk = _dot(hb, wk_ref[...])

    cos, sin_lo, sin_hi = cos_ref[0], sin_lo_ref[0], sin_hi_ref[0]
    tile8 = lambda t: jnp.concatenate([t] * HEADS, axis=1)
    qn = _rms(zr[:, R_QL:R_QL + Q_LORA], rows_ref[ROW_QG:ROW_QG + 1, 0:Q_LORA]).astype(BF16)
    m_scale = MLA_QK ** -0.5 * LOG2E
    zt = _dot_nt(wuqt_ref[...], qn)
    rot = MLA_ROPE // 2
    down = jnp.concatenate([zt[rot:], zt[:rot]], axis=0)
    up = jnp.concatenate([zt[-rot:], zt[:-rot]], axis=0)
    tile8_rows = lambda t: jnp.concatenate([t] * HEADS, axis=0)
    qmt = (zt * tile8_rows(cos.T) + down * tile8_rows(sin_lo.T) + up * tile8_rows(sin_hi.T)) * m_scale
    qmt_ref[0] = qmt.astype(BF16)
    kvn = _rms(zr[:, R_KVL:R_KVL + KV_LORA], rows_ref[ROW_KVG:ROW_KVG + 1, 0:KV_LORA]).astype(BF16)
    krope = rope(misc * rows_ref[ROW_ROPE:ROW_ROPE + 1, 0:LANES], cos, sin_lo, sin_hi)
    km_ref[0] = (_dot(kvn, wkk_ref[...]) + tile8(krope)).astype(BF16)
    store_value_blocks(vmt_ref, _dot_nt(wkvt_ref[...], kvn))
    store_value_blocks(vft_ref, _dot_nt(wvt_ref[...], hb))
    sgf_ref[0] = _silu(zr[:, R_FG:R_FG + FOX_W]).astype(BF16)
    sgm_ref[0] = _silu(zr[:, R_MG:R_MG + FOX_W]).astype(BF16)

    chi, cmid, clo = _split3(cum * LOG2E)
    lane = lax.broadcasted_iota(jnp.int32, (ts, LANES), 1)
    cpieces = jnp.where(
        lane < HEADS, chi.astype(F32),
        jnp.where(lane < 2 * HEADS, pltpu.roll(cmid.astype(F32), HEADS, 1),
                  jnp.where(lane < 3 * HEADS, pltpu.roll(clo.astype(F32), 2 * HEADS, 1), 0.0)))
    cb = cpieces.astype(BF16)
    aug_k = _dot(cb, ek_ref[...]) + rows_ref[ROW_KONES:ROW_KONES + 1, :]
    for hd in range(HEADS):
        pair = slice((hd // 2) * LANES, (hd // 2 + 1) * LANES)
        blk = slice(hd * LANES, (hd + 1) * LANES)
        own = (lane < HEAD_DIM) if hd % 2 == 0 else (lane >= HEAD_DIM)
        kf_ref[0, :, blk] = jnp.where(own, zk[:, pair], aug_k[:, blk]).astype(BF16)
    ones_cols = jnp.concatenate([qones_ref[...]] * (ts // LANES), axis=1)
    aug_qt = _dot_nt(eqt_ref[...], cb) + ones_cols
    for hd in range(HEADS):
        data_rows = zqt[hd * HEAD_DIM:(hd + 1) * HEAD_DIM, :].astype(BF16)
        first = slice(hd * LANES, hd * LANES + HEAD_DIM)
        second = slice(hd * LANES + HEAD_DIM, (hd + 1) * LANES)
        if hd % 2 == 0:
            qft_ref[0, first, :] = data_rows
            qft_ref[0, second, :] = aug_qt[second, :].astype(BF16)
        else:
            qft_ref[0, first, :] = aug_qt[first, :].astype(BF16)
            qft_ref[0, second, :] = data_rows


def _resident(shape):
    nd = len(shape)
    return pl.BlockSpec(shape, lambda b, s: (0,) * nd, pipeline_mode=pl.Buffered(1))


def _layer_slab(stacked, layer):
    nd = stacked.ndim - 1
    return pl.BlockSpec((None,) + stacked.shape[1:], lambda b, s: (layer,) + (0,) * nd,
                        pipeline_mode=pl.Buffered(1))


def _mod_spec(layer):
    return pl.BlockSpec((None, 1, 1, 3 * D_MODEL), lambda b, s: (layer, b, 0, 0))


def _project(x, prev, layer, mod4, norm_g3, w, consts, tables):
    batch, seq, _ = x.shape
    ts = SEQ_TILE
    fused = prev is not None
    row_tile = lambda width: pl.BlockSpec((1, ts, width), lambda b, s: (b, s, 0))
    wide = jax.ShapeDtypeStruct((batch, seq, HEAD_BLOCKS), BF16)
    narrow = jax.ShapeDtypeStruct((batch, seq, FOX_W), BF16)
    wide_t = jax.ShapeDtypeStruct((batch, seq // ts, HEADS * V_ROWS, ts), BF16)
    vt_tile = pl.BlockSpec((1, 1, HEADS * V_ROWS, ts), lambda b, s: (b, s, 0, 0))
    wide_q = jax.ShapeDtypeStruct((batch, HEAD_BLOCKS, seq), BF16)
    qt_tile = pl.BlockSpec((1, HEAD_BLOCKS, ts), lambda b, s: (b, 0, s))
    layer_weights = [w[name] for name in ("wqt", "wk", "wvt", "wr", "wuqt", "wkk", "wkvt", "rows")]

    args, in_specs = [x], [row_tile(D_MODEL)]
    if fused:
        args += [prev[0], prev[1], w["wout"], mod4]
        in_specs += [row_tile(FOX_W), row_tile(FOX_W), _layer_slab(w["wout"], layer - 1),
                     _mod_spec(layer - 1)]
    args += [mod4, norm_g3] + layer_weights + list(consts) + list(tables)
    in_specs += ([_mod_spec(layer), _layer_slab(norm_g3, layer)]
                 + [_layer_slab(a, layer) for a in layer_weights]
                 + [_resident(a.shape) for a in consts]
                 + [row_tile(LANES)] * 3)
    out_specs = [qt_tile, row_tile(HEAD_BLOCKS), vt_tile, row_tile(FOX_W)] * 2
    out_shape = [wide_q, wide, wide_t, narrow] * 2
    if fused:
        out_specs = [row_tile(D_MODEL)] + out_specs
        out_shape = [jax.ShapeDtypeStruct(x.shape, F32)] + out_shape
    outs = pl.pallas_call(
        functools.partial(_proj_kernel, fused=fused),
        grid=(batch, seq // ts),
        in_specs=in_specs, out_specs=out_specs, out_shape=out_shape,
        scratch_shapes=[pltpu.VMEM((8, LANES), F32)],
        compiler_params=_params(),
        name="out_in_proj" if fused else "in_proj",
    )(*args)
    return (outs[0], outs[1:]) if fused else (x, outs)


def _attn_kernel(qt_ref, k_ref, vt_ref, sg_ref, o_ref, m_ref, acc_ref, ahead_ref, *, chunk):
    tq = qt_ref.shape[2]
    i = pl.program_id(1)
    m_ref[...] = jnp.full(m_ref.shape, NEG_BIG, F32)
    acc_ref[...] = jnp.zeros(acc_ref.shape, F32)

    half = tq // 2
    units = [(h, part) for h in range(HEADS) for part in range(2)]

    def diag_mask(nkeys, part):
        key = lax.broadcasted_iota(jnp.int32, (nkeys, half), 0)
        qry = lax.broadcasted_iota(jnp.int32, (nkeys, half), 1) + part * half
        if chunk == 1:
            return key <= qry
        return (key // chunk) <= (qry // chunk)

    def scores(j, u, nkeys):
        h, part = units[u]
        start = pl.multiple_of(j * tq, tq)
        blk = slice(h * LANES, (h + 1) * LANES)
        return _dot(k_ref[0, pl.ds(start, nkeys), blk],
                    qt_ref[0, blk, part * half:(part + 1) * half])

    ring = ahead_ref.shape[0]
    assert len(units) % ring == 0 and ring > SCORE_LOOKAHEAD

    def kv_step(j, masked):
        def unit_keys(u):
            return half if (masked and units[u][1] == 0 and u >= SCORE_LOOKAHEAD) else tq

        for u, (h, part) in enumerate(units):
            cols = slice(part * half, (part + 1) * half)
            nkeys = unit_keys(u)
            ahead = u + SCORE_LOOKAHEAD
            if ahead < len(units):
                ahead_ref[ahead % ring, 0:unit_keys(ahead), :] = scores(j, ahead, unit_keys(ahead))
            elif not masked:
                ahead_ref[ahead % ring] = scores(j + 1, ahead - len(units), tq)

            st = ahead_ref[u % ring, 0:nkeys, :]
            if masked:
                st = jnp.where(diag_mask(nkeys, part), st, NEG_BIG)
            m_prev = m_ref[h, :, cols]
            m_new = jnp.maximum(m_prev, jnp.max(st, axis=0, keepdims=True))
            alpha = jnp.exp2(m_prev - m_new)
            p = jnp.exp2(st - jnp.tile(m_new, (nkeys // 8, 1)))
            vt = vt_ref[0, j, h * V_ROWS:(h + 1) * V_ROWS, 0:nkeys]
            acc_ref[h, :, cols] = (acc_ref[h, :, cols] * jnp.tile(alpha, (V_ROWS // 8, 1))
                                   + _dot(vt, p.astype(BF16)))
            m_ref[h, :, cols] = m_new

    for s in range(SCORE_LOOKAHEAD):
        ahead_ref[s] = scores(0, s, tq)

    @pl.loop(0, i // 2)
    def _(jj):
        kv_step(2 * jj, False)
        kv_step(2 * jj + 1, False)

    @pl.when(i % 2 == 1)
    def _():
        kv_step(i - 1, False)

    kv_step(i, True)

    for pair in range(HEADS // 2):
        halves = []
        for h in (2 * pair, 2 * pair + 1):
            a = acc_ref[h]
            halves.append(a[0:HEAD_DIM, :] / a[HEAD_DIM:HEAD_DIM + 1, :])
        out = jnp.concatenate(halves, axis=0).T
        sg = sg_ref[0, :, pair * LANES:(pair + 1) * LANES].astype(F32)
        o_ref[0, :, pair * LANES:(pair + 1) * LANES] = (out * sg).astype(o_ref.dtype)


def _attention(qt, k, vt, sg, chunk):
    batch, seq, _ = k.shape
    tq = Q_TILE
    assert qt.shape == (batch, HEAD_BLOCKS, seq)
    assert vt.shape == (batch, seq // tq, HEADS * V_ROWS, tq)
    return pl.pallas_call(
        functools.partial(_attn_kernel, chunk=chunk),
        grid=(batch, seq // tq),
        in_specs=[
            pl.BlockSpec((1, HEAD_BLOCKS, tq), lambda b, i: (b, 0, i)),
            pl.BlockSpec((1, seq, HEAD_BLOCKS), lambda b, i: (b, 0, 0)),
            pl.BlockSpec((1, seq // tq, HEADS * V_ROWS, tq), lambda b, i: (b, 0, 0, 0)),
            pl.BlockSpec((1, tq, FOX_W), lambda b, i: (b, i, 0)),
        ],
        out_specs=pl.BlockSpec((1, tq, FOX_W), lambda b, i: (b, i, 0)),
        out_shape=jax.ShapeDtypeStruct((batch, seq, FOX_W), BF16),
        scratch_shapes=[pltpu.VMEM((HEADS, 8, tq), F32), pltpu.VMEM((HEADS, V_ROWS, tq), F32),
                        pltpu.VMEM((SCORE_RING, tq, tq // 2), F32)],
        compiler_params=_params(),
        name="attn_fox" if chunk == 1 else "attn_mla",
    )(qt, k, vt, sg)


def _final_kernel(x_ref, yf_ref, ym_ref, w_ref, mod_ref, fg_ref, o_ref):
    y = _dot(yf_ref[0], w_ref[0:FOX_W, :]) + _dot(ym_ref[0], w_ref[FOX_W:2 * FOX_W, :])
    gate = mod_ref[0, :, 2 * D_MODEL:3 * D_MODEL]
    o_ref[0] = _rms(x_ref[0] + gate * y, fg_ref[...])


def _final_project(x, yf, ym, wout, mod4, final_g):
    batch, seq, _ = x.shape
    ts = FINAL_TILE
    row_tile = lambda width: pl.BlockSpec((1, ts, width), lambda b, s: (b, s, 0))
    return pl.pallas_call(
        _final_kernel,
        grid=(batch, seq // ts),
        in_specs=[row_tile(D_MODEL), row_tile(FOX_W), row_tile(FOX_W),
                  _layer_slab(wout, DEPTH - 1), _mod_spec(DEPTH - 1), _resident((1, D_MODEL))],
        out_specs=row_tile(D_MODEL),
        out_shape=jax.ShapeDtypeStruct(x.shape, F32),
        compiler_params=_params(),
        name="out_proj_final",
    )(x, yf, ym, wout, mod4, final_g.reshape(1, D_MODEL))


def kernel(x, c, positions, norm_g, w_ada, b_ada, w_in, b_f, q_norm_g, w_uq, kv_norm_g, w_ukv,
           w_out, final_g):
    batch = x.shape[0]
    mod4 = _adaln(c, w_ada, b_ada).reshape(DEPTH, batch, 1, 3 * D_MODEL)
    tables = _rope_tables(positions)
    eq, ek = _aug_matrices()
    ltri = np.tril(np.ones((SEQ_TILE, SEQ_TILE), np.float32))
    consts = (jnp.asarray(eq.T, BF16), jnp.asarray(ek, BF16), jnp.asarray(_q_ones_rows()),
              jnp.asarray(ltri, BF16))
    w = _prep_weights(w_in, w_uq, w_ukv, w_out, b_f, q_norm_g, kv_norm_g)
    norm_g3 = norm_g.reshape(DEPTH, 1, D_MODEL)
    prev = None
    for layer in range(DEPTH):
        x, (qf, kf, vf, sgf, qm, km, vm, sgm) = _project(
            x, prev, layer, mod4, norm_g3, w, consts, tables)
        prev = (_attention(qf, kf, vf, sgf, chunk=1), _attention(qm, km, vm, sgm, chunk=CHUNK))
    return _final_project(x, prev[0], prev[1], w["wout"], mod4, final_g)
```

```python
import functools

import numpy as np
import jax
import jax.numpy as jnp
from jax import lax
from jax.experimental import pallas as pl
from jax.experimental.pallas import tpu as pltpu

D_MODEL = 1024
DEPTH = 4
CHUNK = 64
HEADS = 8
HEAD_DIM = 64
FOX_W = HEADS * HEAD_DIM
MLA_ROPE = 32
MLA_QK = HEAD_DIM + MLA_ROPE
Q_LORA = 256
KV_LORA = 128
ROPE_THETA = 10000.0
EPS = 1e-6

LANES = 128
HEAD_BLOCKS = HEADS * LANES
LOG2E = 1.4426950408889634
NEG_BIG = -1e30

SEQ_TILE = 512
FINAL_TILE = 2048
Q_TILE = 512
SCORE_LOOKAHEAD = 4
SCORE_RING = 8
V_ROWS = 80
VMEM_LIMIT = 56 * 1024 * 1024


R_MISC, R_FG, R_QL, R_KVL, R_MG, R_END = 0, 128, 640, 896, 1024, 1536

ROW_KONES, ROW_BF, ROW_QG, ROW_KVG, ROW_ROPE = 1, 3, 4, 5, 6

F32 = jnp.float32
BF16 = jnp.bfloat16


def _dot(a, b):
    return jnp.dot(a, b, preferred_element_type=F32)


def _dot_nt(a, b):
    return lax.dot_general(a, b, (((1,), (1,)), ((), ())), preferred_element_type=F32)


def _split3(v):
    hi = v.astype(BF16)
    r1 = v - hi.astype(F32)
    mid = r1.astype(BF16)
    lo = (r1 - mid.astype(F32)).astype(BF16)
    return hi, mid, lo


def _rms(v, g):
    return v * lax.rsqrt(jnp.mean(v * v, axis=-1, keepdims=True) + EPS) * g


def _silu(v):
    return v * jax.nn.sigmoid(v)


def _params():
    return pltpu.CompilerParams(
        dimension_semantics=("arbitrary", "arbitrary"), vmem_limit_bytes=VMEM_LIMIT)


def _adaln_kernel(c_ref, w_ref, b_ref, o_ref):
    cv = c_ref[...]
    ca = (cv * jax.nn.sigmoid(cv)).astype(BF16)
    o_ref[0] = _dot(ca, w_ref[0].astype(BF16)) + b_ref[0]


def _adaln(c, w_ada, b_ada):
    batch = c.shape[0]
    n_tiles = 3
    return pl.pallas_call(
        _adaln_kernel,
        grid=(DEPTH, n_tiles),
        in_specs=[
            pl.BlockSpec((batch, D_MODEL), lambda l, n: (0, 0)),
            pl.BlockSpec((1, D_MODEL, D_MODEL), lambda l, n: (l, 0, n)),
            pl.BlockSpec((1, 1, D_MODEL), lambda l, n: (l, 0, n)),
        ],
        out_specs=pl.BlockSpec((1, batch, D_MODEL), lambda l, n: (l, 0, n)),
        out_shape=jax.ShapeDtypeStruct((DEPTH, batch, 3 * D_MODEL), F32),
        compiler_params=_params(),
        name="adaln_mod",
    )(c, w_ada, b_ada.reshape(DEPTH, 1, 3 * D_MODEL))


def _rope_table_kernel(pos_ref, invf_ref, cos_ref, sin_lo_ref, sin_hi_ref):
    ang = pos_ref[0].astype(F32) * invf_ref[...]
    sin = jnp.sin(ang)
    lane = lax.broadcasted_iota(jnp.int32, ang.shape, 1)
    first = lane < HEAD_DIM + MLA_ROPE // 2
    cos_ref[0] = jnp.cos(ang)
    sin_lo_ref[0] = jnp.where(first, -sin, 0.0)
    sin_hi_ref[0] = jnp.where(first, 0.0, sin)


def _rope_tables(positions):
    batch, seq = positions.shape
    inv_freq = 1.0 / (ROPE_THETA ** (jnp.arange(0, MLA_ROPE, 2, dtype=F32) / MLA_ROPE))
    half = MLA_ROPE // 2
    invf_row = jnp.zeros((1, LANES), F32)
    invf_row = invf_row.at[0, HEAD_DIM:HEAD_DIM + half].set(inv_freq)
    invf_row = invf_row.at[0, HEAD_DIM + half:HEAD_DIM + MLA_ROPE].set(inv_freq)
    ts = SEQ_TILE
    out = jax.ShapeDtypeStruct((batch, seq, LANES), F32)
    return pl.pallas_call(
        _rope_table_kernel,
        grid=(batch, seq // ts),
        in_specs=[
            pl.BlockSpec((1, ts, 1), lambda b, s: (b, s, 0)),
            pl.BlockSpec((1, LANES), lambda b, s: (0, 0)),
        ],
        out_specs=[pl.BlockSpec((1, ts, LANES), lambda b, s: (b, s, 0))] * 3,
        out_shape=[out, out, out],
        compiler_params=_params(),
        name="rope_tables",
    )(positions.reshape(batch, seq, 1), invf_row)


def _fox_aug_lane(h):
    return h * LANES + (HEAD_DIM if h % 2 == 0 else 0)


def _const_rows():
    rows = np.zeros((8, HEAD_BLOCKS), np.float32)
    for h in range(HEADS):
        base = _fox_aug_lane(h)
        rows[ROW_KONES, base:base + 3] = 1.0
    rows[ROW_ROPE, HEAD_DIM:MLA_QK] = 1.0
    return rows


def _q_ones_rows():
    col = np.zeros((HEAD_BLOCKS, LANES), np.float32)
    for h in range(HEADS):
        base = _fox_aug_lane(h)
        col[base + 3:base + 6, :] = 1.0
    return col


def _aug_matrices():
    eq = np.zeros((LANES, HEAD_BLOCKS), np.float32)
    ek = np.zeros((LANES, HEAD_BLOCKS), np.float32)
    for p in range(3):
        for h in range(HEADS):
            eq[p * HEADS + h, _fox_aug_lane(h) + p] = 1.0
            ek[p * HEADS + h, _fox_aug_lane(h) + 3 + p] = -1.0
    return eq, ek


def _prep_weights(w_in, w_uq, w_ukv, w_out, b_f, q_norm_g, kv_norm_g):
    o = np.cumsum([0, FOX_W, FOX_W, FOX_W, HEADS, FOX_W, Q_LORA, KV_LORA, MLA_ROPE, FOX_W])
    w_in = w_in.astype(BF16)
    w_fq, w_fk, w_fv, w_ff, w_fg, w_ql, w_kvl, w_kr, w_mg = (
        w_in[:, :, o[i]:o[i + 1]] for i in range(9))
    gap = lambda width: jnp.zeros((DEPTH, D_MODEL, width), BF16)
    misc = [w_ff, gap(HEAD_DIM - HEADS), w_kr, gap(LANES - MLA_QK)]
    wr = jnp.concatenate(misc + [w_fg, w_ql, w_kvl, w_mg], axis=2)

    uq = w_uq.reshape(DEPTH, Q_LORA, HEADS, MLA_QK)
    wuq = jnp.pad(uq, ((0, 0), (0, 0), (0, 0), (0, LANES - MLA_QK)))
    ukv = w_ukv.reshape(DEPTH, KV_LORA, HEADS, 2 * HEAD_DIM)
    wkk = jnp.pad(ukv[..., :HEAD_DIM], ((0, 0), (0, 0), (0, 0), (0, LANES - HEAD_DIM)))
    wkvt = jnp.swapaxes(ukv[..., HEAD_DIM:].reshape(DEPTH, KV_LORA, FOX_W), 1, 2)

    rows = jnp.broadcast_to(jnp.asarray(_const_rows()), (DEPTH, 8, HEAD_BLOCKS))
    rows = rows.at[:, ROW_BF, :HEADS].set(b_f)
    rows = rows.at[:, ROW_QG, :Q_LORA].set(q_norm_g)
    rows = rows.at[:, ROW_KVG, :KV_LORA].set(kv_norm_g)
    return dict(
        wqt=jnp.swapaxes(w_fq, 1, 2).astype(BF16),
        wk=w_fk.astype(BF16),
        wvt=jnp.swapaxes(w_fv, 1, 2).astype(BF16),
        wr=wr,
        wuqt=jnp.swapaxes(wuq.reshape(DEPTH, Q_LORA, HEAD_BLOCKS), 1, 2).astype(BF16),
        wkk=wkk.reshape(DEPTH, KV_LORA, HEAD_BLOCKS).astype(BF16),
        wkvt=wkvt.astype(BF16),
        rows=rows, wout=w_out.astype(BF16))


def _proj_kernel(*refs, fused):
    if fused:
        x_ref, yf_ref, ym_ref, wout_ref, modp_ref, *refs = refs
    else:
        x_ref, *refs = refs
    (mod_ref, g_ref, wqt_ref, wk_ref, wvt_ref, wr_ref, wuqt_ref, wkk_ref, wkvt_ref, rows_ref,
     eqt_ref, ek_ref, qones_ref, ltri_ref, cos_ref, sin_lo_ref, sin_hi_ref, *outs) = refs
    if fused:
        xo_ref, *outs = outs
    qft_ref, kf_ref, vft_ref, sgf_ref, qmt_ref, km_ref, vmt_ref, sgm_ref, carry_ref = outs

    ts = x_ref.shape[1]
    pad_rows = lax.broadcasted_iota(jnp.int32, (V_ROWS - HEAD_DIM, ts), 0)
    ones_then_zeros = jnp.where(pad_rows == 0, 1.0, 0.0).astype(BF16)

    def store_value_blocks(out_ref, vt):
        for h in range(HEADS):
            out_ref[0, 0, h * V_ROWS:h * V_ROWS + HEAD_DIM, :] = (
                vt[h * HEAD_DIM:(h + 1) * HEAD_DIM, :].astype(BF16))
            out_ref[0, 0, h * V_ROWS + HEAD_DIM:(h + 1) * V_ROWS, :] = ones_then_zeros

    def rope(t, cos, sin_lo, sin_hi):
        width = t.shape[1]
        down = pltpu.roll(t, width - MLA_ROPE // 2, 1)
        up = pltpu.roll(t, MLA_ROPE // 2, 1)
        return t * cos + down * sin_lo + up * sin_hi

    @pl.when(pl.program_id(1) == 0)
    def _():
        carry_ref[...] = jnp.zeros_like(carry_ref)

    xs = x_ref[0]
    if fused:
        y = (_dot(yf_ref[0], wout_ref[0:FOX_W, :]) + _dot(ym_ref[0], wout_ref[FOX_W:2 * FOX_W, :]))
        xs = xs + modp_ref[0, :, 2 * D_MODEL:3 * D_MODEL] * y
        xo_ref[0] = xs
    shift = mod_ref[0, :, 0:D_MODEL]
    scale = mod_ref[0, :, D_MODEL:2 * D_MODEL]
    h = _rms(xs, g_ref[...]) * (1.0 + scale) + shift
    hb = h.astype(BF16)

    zr = _dot(hb, wr_ref[...])
    misc = zr[:, R_MISC:R_MISC + LANES]
    q_scale = HEAD_DIM ** -0.5 * LOG2E
    zqt = _dot_nt(wqt_ref[...], hb) * q_scale

    ff = misc + rows_ref[ROW_BF:ROW_BF + 1, 0:LANES]
    lf = jnp.minimum(ff, 0.0) - jnp.log1p(jnp.exp(-jnp.abs(ff)))
    hi, mid, lo = _split3(lf)
    ltri = ltri_ref[...]
    cum = _dot(ltri, hi) + _dot(ltri, mid) + _dot(ltri, lo) + carry_ref[0:1, :]
    carry_ref[...] = jnp.broadcast_to(cum[ts - 1:ts, :], carry_ref.shape)

    zk = _dot(hb, wk_ref[...])

    cos, sin_lo, sin_hi = cos_ref[0], sin_lo_ref[0], sin_hi_ref[0]
    tile8 = lambda t: jnp.concatenate([t] * HEADS, axis=1)
    qn = _rms(zr[:, R_QL:R_QL + Q_LORA], rows_ref[ROW_QG:ROW_QG + 1, 0:Q_LORA]).astype(BF16)
    m_scale = MLA_QK ** -0.5 * LOG2E
    zt = _dot_nt(wuqt_ref[...], qn)
    rot = MLA_ROPE // 2
    down = jnp.concatenate([zt[rot:], zt[:rot]], axis=0)
    up = jnp.concatenate([zt[-rot:], zt[:-rot]], axis=0)
    tile8_rows = lambda t: jnp.concatenate([t] * HEADS, axis=0)
    qmt = (zt * tile8_rows(cos.T) + down * tile8_rows(sin_lo.T) + up * tile8_rows(sin_hi.T)) * m_scale
    qmt_ref[0] = qmt.astype(BF16)
    kvn = _rms(zr[:, R_KVL:R_KVL + KV_LORA], rows_ref[ROW_KVG:ROW_KVG + 1, 0:KV_LORA]).astype(BF16)
    krope = rope(misc * rows_ref[ROW_ROPE:ROW_ROPE + 1, 0:LANES], cos, sin_lo, sin_hi)
    km_ref[0] = (_dot(kvn, wkk_ref[...]) + tile8(krope)).astype(BF16)
    store_value_blocks(vmt_ref, _dot_nt(wkvt_ref[...], kvn))
    store_value_blocks(vft_ref, _dot_nt(wvt_ref[...], hb))
    sgf_ref[0] = _silu(zr[:, R_FG:R_FG + FOX_W]).astype(BF16)
    sgm_ref[0] = _silu(zr[:, R_MG:R_MG + FOX_W]).astype(BF16)

    chi, cmid, clo = _split3(cum * LOG2E)
    lane = lax.broadcasted_iota(jnp.int32, (ts, LANES), 1)
    cpieces = jnp.where(
        lane < HEADS, chi.astype(F32),
        jnp.where(lane < 2 * HEADS, pltpu.roll(cmid.astype(F32), HEADS, 1),
                  jnp.where(lane < 3 * HEADS, pltpu.roll(clo.astype(F32), 2 * HEADS, 1), 0.0)))
    cb = cpieces.astype(BF16)
    aug_k = _dot(cb, ek_ref[...]) + rows_ref[ROW_KONES:ROW_KONES + 1, :]
    for hd in range(HEADS):
        pair = slice((hd // 2) * LANES, (hd // 2 + 1) * LANES)
        blk = slice(hd * LANES, (hd + 1) * LANES)
        own = (lane < HEAD_DIM) if hd % 2 == 0 else (lane >= HEAD_DIM)
        kf_ref[0, :, blk] = jnp.where(own, zk[:, pair], aug_k[:, blk]).astype(BF16)
    ones_cols = jnp.concatenate([qones_ref[...]] * (ts // LANES), axis=1)
    aug_qt = _dot_nt(eqt_ref[...], cb) + ones_cols
    for hd in range(HEADS):
        data_rows = zqt[hd * HEAD_DIM:(hd + 1) * HEAD_DIM, :].astype(BF16)
        first = slice(hd * LANES, hd * LANES + HEAD_DIM)
        second = slice(hd * LANES + HEAD_DIM, (hd + 1) * LANES)
        if hd % 2 == 0:
            qft_ref[0, first, :] = data_rows
            qft_ref[0, second, :] = aug_qt[second, :].astype(BF16)
        else:
            qft_ref[0, first, :] = aug_qt[first, :].astype(BF16)
            qft_ref[0, second, :] = data_rows


def _resident(shape):
    nd = len(shape)
    return pl.BlockSpec(shape, lambda b, s: (0,) * nd, pipeline_mode=pl.Buffered(1))


def _layer_slab(stacked, layer):
    nd = stacked.ndim - 1
    return pl.BlockSpec((None,) + stacked.shape[1:], lambda b, s: (layer,) + (0,) * nd,
                        pipeline_mode=pl.Buffered(1))


def _mod_spec(layer):
    return pl.BlockSpec((None, 1, 1, 3 * D_MODEL), lambda b, s: (layer, b, 0, 0))


def _project(x, prev, layer, mod4, norm_g3, w, consts, tables):
    batch, seq, _ = x.shape
    ts = SEQ_TILE
    fused = prev is not None
    row_tile = lambda width: pl.BlockSpec((1, ts, width), lambda b, s: (b, s, 0))
    wide = jax.ShapeDtypeStruct((batch, seq, HEAD_BLOCKS), BF16)
    narrow = jax.ShapeDtypeStruct((batch, seq, FOX_W), BF16)
    wide_t = jax.ShapeDtypeStruct((batch, seq // ts, HEADS * V_ROWS, ts), BF16)
    vt_tile = pl.BlockSpec((1, 1, HEADS * V_ROWS, ts), lambda b, s: (b, s, 0, 0))
    wide_q = jax.ShapeDtypeStruct((batch, HEAD_BLOCKS, seq), BF16)
    qt_tile = pl.BlockSpec((1, HEAD_BLOCKS, ts), lambda b, s: (b, 0, s))
    layer_weights = [w[name] for name in ("wqt", "wk", "wvt", "wr", "wuqt", "wkk", "wkvt", "rows")]

    args, in_specs = [x], [row_tile(D_MODEL)]
    if fused:
        args += [prev[0], prev[1], w["wout"], mod4]
        in_specs += [row_tile(FOX_W), row_tile(FOX_W), _layer_slab(w["wout"], layer - 1),
                     _mod_spec(layer - 1)]
    args += [mod4, norm_g3] + layer_weights + list(consts) + list(tables)
    in_specs += ([_mod_spec(layer), _layer_slab(norm_g3, layer)]
                 + [_layer_slab(a, layer) for a in layer_weights]
                 + [_resident(a.shape) for a in consts]
                 + [row_tile(LANES)] * 3)
    out_specs = [qt_tile, row_tile(HEAD_BLOCKS), vt_tile, row_tile(FOX_W)] * 2
    out_shape = [wide_q, wide, wide_t, narrow] * 2
    if fused:
        out_specs = [row_tile(D_MODEL)] + out_specs
        out_shape = [jax.ShapeDtypeStruct(x.shape, F32)] + out_shape
    outs = pl.pallas_call(
        functools.partial(_proj_kernel, fused=fused),
        grid=(batch, seq // ts),
        in_specs=in_specs, out_specs=out_specs, out_shape=out_shape,
        scratch_shapes=[pltpu.VMEM((8, LANES), F32)],
        compiler_params=_params(),
        name="out_in_proj" if fused else "in_proj",
    )(*args)
    return (outs[0], outs[1:]) if fused else (x, outs)


def _attn_kernel(qt_ref, k_ref, vt_ref, sg_ref, o_ref, m_ref, acc_ref, ahead_ref, *, chunk):
    tq = qt_ref.shape[2]
    i = pl.program_id(1)
    m_ref[...] = jnp.full(m_ref.shape, NEG_BIG, F32)
    acc_ref[...] = jnp.zeros(acc_ref.shape, F32)

    half = tq // 2
    units = [(h, part) for h in range(HEADS) for part in range(2)]

    def diag_mask(nkeys, part):
        key = lax.broadcasted_iota(jnp.int32, (nkeys, half), 0)
        qry = lax.broadcasted_iota(jnp.int32, (nkeys, half), 1) + part * half
        if chunk == 1:
            return key <= qry
        return (key // chunk) <= (qry // chunk)

    def scores(j, u, nkeys):
        h, part = units[u]
        start = pl.multiple_of(j * tq, tq)
        blk = slice(h * LANES, (h + 1) * LANES)
        return _dot(k_ref[0, pl.ds(start, nkeys), blk],
                    qt_ref[0, blk, part * half:(part + 1) * half])

    ring = ahead_ref.shape[0]
    assert len(units) % ring == 0 and ring > SCORE_LOOKAHEAD

    def kv_step(j, masked):
        def unit_keys(u):
            return half if (masked and units[u][1] == 0 and u >= SCORE_LOOKAHEAD) else tq

        for u, (h, part) in enumerate(units):
            cols = slice(part * half, (part + 1) * half)
            nkeys = unit_keys(u)
            ahead = u + SCORE_LOOKAHEAD
            if ahead < len(units):
                ahead_ref[ahead % ring, 0:unit_keys(ahead), :] = scores(j, ahead, unit_keys(ahead))
            elif not masked:
                ahead_ref[ahead % ring] = scores(j + 1, ahead - len(units), tq)

            st = ahead_ref[u % ring, 0:nkeys, :]
            if masked:
                st = jnp.where(diag_mask(nkeys, part), st, NEG_BIG)
            m_prev = m_ref[h, :, cols]
            m_new = jnp.maximum(m_prev, jnp.max(st, axis=0, keepdims=True))
            alpha = jnp.exp2(m_prev - m_new)
            p = jnp.exp2(st - jnp.tile(m_new, (nkeys // 8, 1)))
            vt = vt_ref[0, j, h * V_ROWS:(h + 1) * V_ROWS, 0:nkeys]
            acc_ref[h, :, cols] = (acc_ref[h, :, cols] * jnp.tile(alpha, (V_ROWS // 8, 1))
                                   + _dot(vt, p.astype(BF16)))
            m_ref[h, :, cols] = m_new

    for s in range(SCORE_LOOKAHEAD):
        ahead_ref[s] = scores(0, s, tq)

    @pl.loop(0, i // 2)
    def _(jj):
        kv_step(2 * jj, False)
        kv_step(2 * jj + 1, False)

    @pl.when(i % 2 == 1)
    def _():
        kv_step(i - 1, False)

    kv_step(i, True)

    for pair in range(HEADS // 2):
        halves = []
        for h in (2 * pair, 2 * pair + 1):
            a = acc_ref[h]
            halves.append(a[0:HEAD_DIM, :] / a[HEAD_DIM:HEAD_DIM + 1, :])
        out = jnp.concatenate(halves, axis=0).T
        sg = sg_ref[0, :, pair * LANES:(pair + 1) * LANES].astype(F32)
        o_ref[0, :, pair * LANES:(pair + 1) * LANES] = (out * sg).astype(o_ref.dtype)


def _attention(qt, k, vt, sg, chunk):
    batch, seq, _ = k.shape
    tq = Q_TILE
    assert qt.shape == (batch, HEAD_BLOCKS, seq)
    assert vt.shape == (batch, seq // tq, HEADS * V_ROWS, tq)
    return pl.pallas_call(
        functools.partial(_attn_kernel, chunk=chunk),
        grid=(batch, seq // tq),
        in_specs=[
            pl.BlockSpec((1, HEAD_BLOCKS, tq), lambda b, i: (b, 0, i)),
            pl.BlockSpec((1, seq, HEAD_BLOCKS), lambda b, i: (b, 0, 0)),
            pl.BlockSpec((1, seq // tq, HEADS * V_ROWS, tq), lambda b, i: (b, 0, 0, 0)),
            pl.BlockSpec((1, tq, FOX_W), lambda b, i: (b, i, 0)),
        ],
        out_specs=pl.BlockSpec((1, tq, FOX_W), lambda b, i: (b, i, 0)),
        out_shape=jax.ShapeDtypeStruct((batch, seq, FOX_W), BF16),
        scratch_shapes=[pltpu.VMEM((HEADS, 8, tq), F32), pltpu.VMEM((HEADS, V_ROWS, tq), F32),
                        pltpu.VMEM((SCORE_RING, tq, tq // 2), F32)],
        compiler_params=_params(),
        name="attn_fox" if chunk == 1 else "attn_mla",
    )(qt, k, vt, sg)


def _final_kernel(x_ref, yf_ref, ym_ref, w_ref, mod_ref, fg_ref, o_ref):
    y = _dot(yf_ref[0], w_ref[0:FOX_W, :]) + _dot(ym_ref[0], w_ref[FOX_W:2 * FOX_W, :])
    gate = mod_ref[0, :, 2 * D_MODEL:3 * D_MODEL]
    o_ref[0] = _rms(x_ref[0] + gate * y, fg_ref[...])


def _final_project(x, yf, ym, wout, mod4, final_g):
    batch, seq, _ = x.shape
    ts = FINAL_TILE
    row_tile = lambda width: pl.BlockSpec((1, ts, width), lambda b, s: (b, s, 0))
    return pl.pallas_call(
        _final_kernel,
        grid=(batch, seq // ts),
        in_specs=[row_tile(D_MODEL), row_tile(FOX_W), row_tile(FOX_W),
                  _layer_slab(wout, DEPTH - 1), _mod_spec(DEPTH - 1), _resident((1, D_MODEL))],
        out_specs=row_tile(D_MODEL),
        out_shape=jax.ShapeDtypeStruct(x.shape, F32),
        compiler_params=_params(),
        name="out_proj_final",
    )(x, yf, ym, wout, mod4, final_g.reshape(1, D_MODEL))


def kernel(x, c, positions, norm_g, w_ada, b_ada, w_in, b_f, q_norm_g, w_uq, kv_norm_g, w_ukv,
           w_out, final_g):
    batch = x.shape[0]
    mod4 = _adaln(c, w_ada, b_ada).reshape(DEPTH, batch, 1, 3 * D_MODEL)
    tables = _rope_tables(positions)
    eq, ek = _aug_matrices()
    ltri = np.tril(np.ones((SEQ_TILE, SEQ_TILE), np.float32))
    consts = (jnp.asarray(eq.T, BF16), jnp.asarray(ek, BF16), jnp.asarray(_q_ones_rows()),
              jnp.asarray(ltri, BF16))
    w = _prep_weights(w_in, w_uq, w_ukv, w_out, b_f, q_norm_g, kv_norm_g)
    norm_g3 = norm_g.reshape(DEPTH, 1, D_MODEL)
    prev = None
    for layer in range(DEPTH):
        x, (qf, kf, vf, sgf, qm, km, vm, sgm) = _project(
            x, prev, layer, mod4, norm_g3, w, consts, tables)
        prev = (_attention(qf, kf, vf, sgf, chunk=1), _attention(qm, km, vm, sgm, chunk=CHUNK))
    return _final_project(x, prev[0], prev[1], w["wout"], mod4, final_g)
```

```python
import functools

import numpy as np
import jax
import jax.numpy as jnp
from jax import lax
from jax.experimental import pallas as pl
from jax.experimental.pallas import tpu as pltpu

D_MODEL = 1024
DEPTH = 4
CHUNK = 64
HEADS = 8
HEAD_DIM = 64
FOX_W = HEADS * HEAD_DIM
MLA_ROPE = 32
MLA_QK = HEAD_DIM + MLA_ROPE
Q_LORA = 256
KV_LORA = 128
ROPE_THETA = 10000.0
EPS = 1e-6

LANES = 128
HEAD_BLOCKS = HEADS * LANES
LOG2E = 1.4426950408889634
NEG_BIG = -1e30

SEQ_TILE = 512
FINAL_TILE = 1024
Q_TILE = 512
SCORE_LOOKAHEAD = 4
SCORE_RING = 8
V_ROWS = 80
VMEM_LIMIT = 56 * 1024 * 1024


R_MISC, R_FG, R_QL, R_KVL, R_MG, R_END = 0, 128, 640, 896, 1024, 1536

ROW_KONES, ROW_BF, ROW_QG, ROW_KVG, ROW_ROPE = 1, 3, 4, 5, 6

F32 = jnp.float32
BF16 = jnp.bfloat16


def _dot(a, b):
    return jnp.dot(a, b, preferred_element_type=F32)


def _dot_nt(a, b):
    return lax.dot_general(a, b, (((1,), (1,)), ((), ())), preferred_element_type=F32)


def _split3(v):
    hi = v.astype(BF16)
    r1 = v - hi.astype(F32)
    mid = r1.astype(BF16)
    lo = (r1 - mid.astype(F32)).astype(BF16)
    return hi, mid, lo


def _rms(v, g):
    return v * lax.rsqrt(jnp.mean(v * v, axis=-1, keepdims=True) + EPS) * g


def _silu(v):
    return v * jax.nn.sigmoid(v)


def _params(grid_rank=2):
    return pltpu.CompilerParams(
        dimension_semantics=("arbitrary",) * grid_rank, vmem_limit_bytes=VMEM_LIMIT)


def _adaln_kernel(c_ref, w_ref, b_ref, o_ref):
    cv = c_ref[...]
    ca = (cv * jax.nn.sigmoid(cv)).astype(BF16)
    o_ref[0] = _dot(ca, w_ref[0].astype(BF16)) + b_ref[0]


def _adaln(c, w_ada, b_ada):
    batch = c.shape[0]
    n_tiles = 3
    return pl.pallas_call(
        _adaln_kernel,
        grid=(DEPTH, n_tiles),
        in_specs=[
            pl.BlockSpec((batch, D_MODEL), lambda l, n: (0, 0)),
            pl.BlockSpec((1, D_MODEL, D_MODEL), lambda l, n: (l, 0, n)),
            pl.BlockSpec((1, 1, D_MODEL), lambda l, n: (l, 0, n)),
        ],
        out_specs=pl.BlockSpec((1, batch, D_MODEL), lambda l, n: (l, 0, n)),
        out_shape=jax.ShapeDtypeStruct((DEPTH, batch, 3 * D_MODEL), F32),
        compiler_params=_params(),
        name="adaln_mod",
    )(c, w_ada, b_ada.reshape(DEPTH, 1, 3 * D_MODEL))


def _rope_table_kernel(pos_ref, invf_ref, cos_ref, sin_lo_ref, sin_hi_ref):
    ang = pos_ref[0].astype(F32) * invf_ref[...]
    sin = jnp.sin(ang)
    lane = lax.broadcasted_iota(jnp.int32, ang.shape, 1)
    first = lane < HEAD_DIM + MLA_ROPE // 2
    cos_ref[0] = jnp.cos(ang)
    sin_lo_ref[0] = jnp.where(first, -sin, 0.0)
    sin_hi_ref[0] = jnp.where(first, 0.0, sin)


def _rope_tables(positions):
    batch, seq = positions.shape
    inv_freq = 1.0 / (ROPE_THETA ** (jnp.arange(0, MLA_ROPE, 2, dtype=F32) / MLA_ROPE))
    half = MLA_ROPE // 2
    invf_row = jnp.zeros((1, LANES), F32)
    invf_row = invf_row.at[0, HEAD_DIM:HEAD_DIM + half].set(inv_freq)
    invf_row = invf_row.at[0, HEAD_DIM + half:HEAD_DIM + MLA_ROPE].set(inv_freq)
    ts = SEQ_TILE
    out = jax.ShapeDtypeStruct((batch, seq, LANES), F32)
    return pl.pallas_call(
        _rope_table_kernel,
        grid=(batch, seq // ts),
        in_specs=[
            pl.BlockSpec((1, ts, 1), lambda b, s: (b, s, 0)),
            pl.BlockSpec((1, LANES), lambda b, s: (0, 0)),
        ],
        out_specs=[pl.BlockSpec((1, ts, LANES), lambda b, s: (b, s, 0))] * 3,
        out_shape=[out, out, out],
        compiler_params=_params(),
        name="rope_tables",
    )(positions.reshape(batch, seq, 1), invf_row)


def _fox_aug_lane(h):
    return h * LANES + (HEAD_DIM if h % 2 == 0 else 0)


def _const_rows():
    rows = np.zeros((8, HEAD_BLOCKS), np.float32)
    for h in range(HEADS):
        base = _fox_aug_lane(h)
        rows[ROW_KONES, base:base + 3] = 1.0
    rows[ROW_ROPE, HEAD_DIM:MLA_QK] = 1.0
    return rows


def _q_ones_rows():
    col = np.zeros((HEAD_BLOCKS, LANES), np.float32)
    for h in range(HEADS):
        base = _fox_aug_lane(h)
        col[base + 3:base + 6, :] = 1.0
    return col


def _aug_matrices():
    eq = np.zeros((LANES, HEAD_BLOCKS), np.float32)
    ek = np.zeros((LANES, HEAD_BLOCKS), np.float32)
    for p in range(3):
        for h in range(HEADS):
            eq[p * HEADS + h, _fox_aug_lane(h) + p] = 1.0
            ek[p * HEADS + h, _fox_aug_lane(h) + 3 + p] = -1.0
    return eq, ek


def _prep_weights(w_in, w_uq, w_ukv, w_out, b_f, q_norm_g, kv_norm_g):
    o = np.cumsum([0, FOX_W, FOX_W, FOX_W, HEADS, FOX_W, Q_LORA, KV_LORA, MLA_ROPE, FOX_W])
    w_in = w_in.astype(BF16)
    w_fq, w_fk, w_fv, w_ff, w_fg, w_ql, w_kvl, w_kr, w_mg = (
        w_in[:, :, o[i]:o[i + 1]] for i in range(9))
    misc = jnp.zeros((DEPTH, D_MODEL, LANES), BF16)
    misc = misc.at[:, :, 0:HEADS].set(w_ff).at[:, :, HEAD_DIM:MLA_QK].set(w_kr)
    wr = jnp.concatenate([misc, w_fg, w_ql, w_kvl, w_mg], axis=2)

    uq = w_uq.reshape(DEPTH, Q_LORA, HEADS, MLA_QK)
    wuq = jnp.pad(uq, ((0, 0), (0, 0), (0, 0), (0, LANES - MLA_QK)))
    ukv = w_ukv.reshape(DEPTH, KV_LORA, HEADS, 2 * HEAD_DIM)
    wkk = jnp.pad(ukv[..., :HEAD_DIM], ((0, 0), (0, 0), (0, 0), (0, LANES - HEAD_DIM)))
    wkvt = jnp.swapaxes(ukv[..., HEAD_DIM:].reshape(DEPTH, KV_LORA, FOX_W), 1, 2)

    rows = jnp.broadcast_to(jnp.asarray(_const_rows()), (DEPTH, 8, HEAD_BLOCKS))
    rows = rows.at[:, ROW_BF, :HEADS].set(b_f)
    rows = rows.at[:, ROW_QG, :Q_LORA].set(q_norm_g)
    rows = rows.at[:, ROW_KVG, :KV_LORA].set(kv_norm_g)
    return dict(
        wqt=jnp.swapaxes(w_fq, 1, 2).astype(BF16),
        wk=w_fk.astype(BF16),
        wvt=jnp.swapaxes(w_fv, 1, 2).astype(BF16),
        wr=wr,
        wuqt=jnp.swapaxes(wuq.reshape(DEPTH, Q_LORA, HEAD_BLOCKS), 1, 2).astype(BF16),
        wkk=wkk.reshape(DEPTH, KV_LORA, HEAD_BLOCKS).astype(BF16),
        wkvt=wkvt.astype(BF16),
        rows=rows, wout=w_out.astype(BF16))


def _proj_kernel(*refs, fused):
    if fused:
        x_ref, yf_ref, ym_ref, wout_ref, modp_ref, *refs = refs
    else:
        x_ref, *refs = refs
    (mod_ref, g_ref, wqt_ref, wk_ref, wvt_ref, wr_ref, wuqt_ref, wkk_ref, wkvt_ref, rows_ref,
     eqt_ref, ek_ref, qones_ref, ltri_ref, cos_ref, sin_lo_ref, sin_hi_ref, *outs) = refs
    if fused:
        xo_ref, *outs = outs
    qt_ref, k_ref, vt_ref, sg_ref, carry_ref = outs
    qft_ref, kf_ref, vft_ref, sgf_ref = (r.at[0] for r in (qt_ref, k_ref, vt_ref, sg_ref))
    qmt_ref, km_ref, vmt_ref, sgm_ref = (r.at[1] for r in (qt_ref, k_ref, vt_ref, sg_ref))

    ts = x_ref.shape[1]
    pad_rows = lax.broadcasted_iota(jnp.int32, (V_ROWS - HEAD_DIM, ts), 0)
    ones_then_zeros = jnp.where(pad_rows == 0, 1.0, 0.0).astype(BF16)

    def store_value_blocks(out_ref, vt):
        for h in range(HEADS):
            out_ref[0, 0, h * V_ROWS:h * V_ROWS + HEAD_DIM, :] = (
                vt[h * HEAD_DIM:(h + 1) * HEAD_DIM, :].astype(BF16))
            out_ref[0, 0, h * V_ROWS + HEAD_DIM:(h + 1) * V_ROWS, :] = ones_then_zeros

    def rope(t, cos, sin_lo, sin_hi):
        width = t.shape[1]
        down = pltpu.roll(t, width - MLA_ROPE // 2, 1)
        up = pltpu.roll(t, MLA_ROPE // 2, 1)
        return t * cos + down * sin_lo + up * sin_hi

    @pl.when(pl.program_id(1) == 0)
    def _():
        carry_ref[...] = jnp.zeros_like(carry_ref)

    xs = x_ref[0]
    if fused:
        y = (_dot(yf_ref[0], wout_ref[0:FOX_W, :]) + _dot(ym_ref[0], wout_ref[FOX_W:2 * FOX_W, :]))
        xs = xs + modp_ref[0, :, 2 * D_MODEL:3 * D_MODEL] * y
        xo_ref[0] = xs
    shift = mod_ref[0, :, 0:D_MODEL]
    scale = mod_ref[0, :, D_MODEL:2 * D_MODEL]
    h = _rms(xs, g_ref[...]) * (1.0 + scale) + shift
    hb = h.astype(BF16)

    zr = _dot(hb, wr_ref[...])
    misc = zr[:, R_MISC:R_MISC + LANES]
    q_scale = HEAD_DIM ** -0.5 * LOG2E
    zqt = _dot_nt(wqt_ref[...], hb) * q_scale

    ff = misc + rows_ref[ROW_BF:ROW_BF + 1, 0:LANES]
    lf = jnp.minimum(ff, 0.0) - jnp.log1p(jnp.exp(-jnp.abs(ff)))
    hi, mid, lo = _split3(lf)
    ltri = ltri_ref[...]
    cum = _dot(ltri, hi) + _dot(ltri, mid) + _dot(ltri, lo) + carry_ref[0:1, :]
    carry_ref[...] = jnp.broadcast_to(cum[ts - 1:ts, :], carry_ref.shape)

    zk = _dot(hb, wk_ref[...])

    cos, sin_lo, sin_hi = cos_ref[0], sin_lo_ref[0], sin_hi_ref[0]
    tile8 = lambda t: jnp.concatenate([t] * HEADS, axis=1)
    qn = _rms(zr[:, R_QL:R_QL + Q_LORA], rows_ref[ROW_QG:ROW_QG + 1, 0:Q_LORA]).astype(BF16)
    m_scale = MLA_QK ** -0.5 * LOG2E
    zt = _dot_nt(wuqt_ref[...], qn)
    rot = MLA_ROPE // 2
    down = jnp.concatenate([zt[rot:], zt[:rot]], axis=0)
    up = jnp.concatenate([zt[-rot:], zt[:-rot]], axis=0)
    tile8_rows = lambda t: jnp.concatenate([t] * HEADS, axis=0)
    qmt = (zt * tile8_rows(cos.T) + down * tile8_rows(sin_lo.T) + up * tile8_rows(sin_hi.T)) * m_scale
    qmt_ref[0] = qmt.astype(BF16)
    kvn = _rms(zr[:, R_KVL:R_KVL + KV_LORA], rows_ref[ROW_KVG:ROW_KVG + 1, 0:KV_LORA]).astype(BF16)
    krope = rope(misc * rows_ref[ROW_ROPE:ROW_ROPE + 1, 0:LANES], cos, sin_lo, sin_hi)
    km_ref[0] = (_dot(kvn, wkk_ref[...]) + tile8(krope)).astype(BF16)
    store_value_blocks(vmt_ref, _dot_nt(wkvt_ref[...], kvn))
    store_value_blocks(vft_ref, _dot_nt(wvt_ref[...], hb))
    sgf_ref[0] = _silu(zr[:, R_FG:R_FG + FOX_W]).astype(BF16)
    sgm_ref[0] = _silu(zr[:, R_MG:R_MG + FOX_W]).astype(BF16)

    chi, cmid, clo = _split3(cum * LOG2E)
    lane = lax.broadcasted_iota(jnp.int32, (ts, LANES), 1)
    cpieces = jnp.where(
        lane < HEADS, chi.astype(F32),
        jnp.where(lane < 2 * HEADS, pltpu.roll(cmid.astype(F32), HEADS, 1),
                  jnp.where(lane < 3 * HEADS, pltpu.roll(clo.astype(F32), 2 * HEADS, 1), 0.0)))
    cb = cpieces.astype(BF16)
    aug_k = _dot(cb, ek_ref[...]) + rows_ref[ROW_KONES:ROW_KONES + 1, :]
    for hd in range(HEADS):
        pair = slice((hd // 2) * LANES, (hd // 2 + 1) * LANES)
        blk = slice(hd * LANES, (hd + 1) * LANES)
        own = (lane < HEAD_DIM) if hd % 2 == 0 else (lane >= HEAD_DIM)
        kf_ref[0, :, blk] = jnp.where(own, zk[:, pair], aug_k[:, blk]).astype(BF16)
    ones_cols = jnp.concatenate([qones_ref[...]] * (ts // LANES), axis=1)
    aug_qt = _dot_nt(eqt_ref[...], cb) + ones_cols
    for hd in range(HEADS):
        data_rows = zqt[hd * HEAD_DIM:(hd + 1) * HEAD_DIM, :].astype(BF16)
        first = slice(hd * LANES, hd * LANES + HEAD_DIM)
        second = slice(hd * LANES + HEAD_DIM, (hd + 1) * LANES)
        if hd % 2 == 0:
            qft_ref[0, first, :] = data_rows
            qft_ref[0, second, :] = aug_qt[second, :].astype(BF16)
        else:
            qft_ref[0, first, :] = aug_qt[first, :].astype(BF16)
            qft_ref[0, second, :] = data_rows


def _resident(shape):
    nd = len(shape)
    return pl.BlockSpec(shape, lambda b, s: (0,) * nd, pipeline_mode=pl.Buffered(1))


def _layer_slab(stacked, layer):
    nd = stacked.ndim - 1
    return pl.BlockSpec((None,) + stacked.shape[1:], lambda b, s: (layer,) + (0,) * nd,
                        pipeline_mode=pl.Buffered(1))


def _mod_spec(layer):
    return pl.BlockSpec((None, 1, 1, 3 * D_MODEL), lambda b, s: (layer, b, 0, 0))


def _project(x, prev, layer, mod4, norm_g3, w, consts, tables):
    batch, seq, _ = x.shape
    ts = SEQ_TILE
    fused = prev is not None
    row_tile = lambda width: pl.BlockSpec((1, ts, width), lambda b, s: (b, s, 0))
    wide = jax.ShapeDtypeStruct((2, batch, seq, HEAD_BLOCKS), BF16)
    narrow = jax.ShapeDtypeStruct((2, batch, seq, FOX_W), BF16)
    both = lambda width: pl.BlockSpec((2, 1, ts, width), lambda b, s: (0, b, s, 0))
    wide_t = jax.ShapeDtypeStruct((2, batch, seq // ts, HEADS * V_ROWS, ts), BF16)
    vt_tile = pl.BlockSpec((2, 1, 1, HEADS * V_ROWS, ts), lambda b, s: (0, b, s, 0, 0))
    wide_q = jax.ShapeDtypeStruct((2, batch, HEAD_BLOCKS, seq), BF16)
    qt_tile = pl.BlockSpec((2, 1, HEAD_BLOCKS, ts), lambda b, s: (0, b, 0, s))
    group_rows = lambda g: pl.BlockSpec((None, 1, ts, FOX_W), lambda b, s: (g, b, s, 0))
    layer_weights = [w[name] for name in ("wqt", "wk", "wvt", "wr", "wuqt", "wkk", "wkvt", "rows")]

    args, in_specs = [x], [row_tile(D_MODEL)]
    if fused:
        args += [prev, prev, w["wout"], mod4]
        in_specs += [group_rows(0), group_rows(1), _layer_slab(w["wout"], layer - 1),
                     _mod_spec(layer - 1)]
    args += [mod4, norm_g3] + layer_weights + list(consts) + list(tables)
    in_specs += ([_mod_spec(layer), _layer_slab(norm_g3, layer)]
                 + [_layer_slab(a, layer) for a in layer_weights]
                 + [_resident(a.shape) for a in consts]
                 + [row_tile(LANES)] * 3)
    out_specs = [qt_tile, both(HEAD_BLOCKS), vt_tile, both(FOX_W)]
    out_shape = [wide_q, wide, wide_t, narrow]
    if fused:
        out_specs = [row_tile(D_MODEL)] + out_specs
        out_shape = [jax.ShapeDtypeStruct(x.shape, F32)] + out_shape
    outs = pl.pallas_call(
        functools.partial(_proj_kernel, fused=fused),
        grid=(batch, seq // ts),
        in_specs=in_specs, out_specs=out_specs, out_shape=out_shape,
        scratch_shapes=[pltpu.VMEM((8, LANES), F32)],
        compiler_params=_params(),
        name="out_in_proj" if fused else "in_proj",
    )(*args)
    return (outs[0], outs[1:]) if fused else (x, outs)


def _attn_kernel(qt_ref, k_ref, vt_ref, sg_ref, o_ref, m_ref, acc_ref, ahead_ref):
    tq = qt_ref.shape[2]
    i = pl.program_id(2)
    chunk_shift = pl.program_id(0) * (CHUNK.bit_length() - 1)
    m_ref[...] = jnp.full(m_ref.shape, NEG_BIG, F32)
    acc_ref[...] = jnp.zeros(acc_ref.shape, F32)

    half = tq // 2
    units = [(h, part) for h in range(HEADS) for part in range(2)]

    def diag_mask(nkeys, part):
        key = lax.broadcasted_iota(jnp.int32, (nkeys, half), 0)
        qry = lax.broadcasted_iota(jnp.int32, (nkeys, half), 1) + part * half
        return jnp.right_shift(key, chunk_shift) <= jnp.right_shift(qry, chunk_shift)

    def scores(j, u, nkeys):
        h, part = units[u]
        start = pl.multiple_of(j * tq, tq)
        blk = slice(h * LANES, (h + 1) * LANES)
        return _dot(k_ref[0, pl.ds(start, nkeys), blk],
                    qt_ref[0, blk, part * half:(part + 1) * half])

    ring = ahead_ref.shape[0]
    assert len(units) % ring == 0 and ring > SCORE_LOOKAHEAD

    def kv_step(j, masked):
        def unit_keys(u):
            return half if (masked and units[u][1] == 0 and u >= SCORE_LOOKAHEAD) else tq

        for u, (h, part) in enumerate(units):
            cols = slice(part * half, (part + 1) * half)
            nkeys = unit_keys(u)
            ahead = u + SCORE_LOOKAHEAD
            if ahead < len(units):
                ahead_ref[ahead % ring, 0:unit_keys(ahead), :] = scores(j, ahead, unit_keys(ahead))
            elif not masked:
                ahead_ref[ahead % ring] = scores(j + 1, ahead - len(units), tq)

            st = ahead_ref[u % ring, 0:nkeys, :]
            if masked:
                st = jnp.where(diag_mask(nkeys, part), st, NEG_BIG)
            m_prev = m_ref[h, :, cols]
            m_new = jnp.maximum(m_prev, jnp.max(st, axis=0, keepdims=True))
            alpha = jnp.exp2(m_prev - m_new)
            p = jnp.exp2(st - jnp.tile(m_new, (nkeys // 8, 1)))
            vt = vt_ref[0, j, h * V_ROWS:(h + 1) * V_ROWS, 0:nkeys]
            acc_ref[h, :, cols] = (acc_ref[h, :, cols] * jnp.tile(alpha, (V_ROWS // 8, 1))
                                   + _dot(vt, p.astype(BF16)))
            m_ref[h, :, cols] = m_new

    for s in range(SCORE_LOOKAHEAD):
        ahead_ref[s] = scores(0, s, tq)

    @pl.loop(0, i // 2)
    def _(jj):
        kv_step(2 * jj, False)
        kv_step(2 * jj + 1, False)

    @pl.when(i % 2 == 1)
    def _():
        kv_step(i - 1, False)

    kv_step(i, True)

    for pair in range(HEADS // 2):
        halves = []
        for h in (2 * pair, 2 * pair + 1):
            a = acc_ref[h]
            halves.append(a[0:HEAD_DIM, :] / a[HEAD_DIM:HEAD_DIM + 1, :])
        out = jnp.concatenate(halves, axis=0).T
        sg = sg_ref[0, :, pair * LANES:(pair + 1) * LANES].astype(F32)
        o_ref[0, :, pair * LANES:(pair + 1) * LANES] = (out * sg).astype(o_ref.dtype)


def _attention(qt, k, vt, sg):
    _, batch, seq, _ = k.shape
    tq = Q_TILE
    assert qt.shape == (2, batch, HEAD_BLOCKS, seq)
    assert vt.shape == (2, batch, seq // tq, HEADS * V_ROWS, tq)
    return pl.pallas_call(
        _attn_kernel,
        grid=(2, batch, seq // tq),
        in_specs=[
            pl.BlockSpec((None, 1, HEAD_BLOCKS, tq), lambda g, b, i: (g, b, 0, i)),
            pl.BlockSpec((None, 1, seq, HEAD_BLOCKS), lambda g, b, i: (g, b, 0, 0)),
            pl.BlockSpec((None, 1, seq // tq, HEADS * V_ROWS, tq), lambda g, b, i: (g, b, 0, 0, 0)),
            pl.BlockSpec((None, 1, tq, FOX_W), lambda g, b, i: (g, b, i, 0)),
        ],
        out_specs=pl.BlockSpec((None, 1, tq, FOX_W), lambda g, b, i: (g, b, i, 0)),
        out_shape=jax.ShapeDtypeStruct((2, batch, seq, FOX_W), BF16),
        scratch_shapes=[pltpu.VMEM((HEADS, 8, tq), F32), pltpu.VMEM((HEADS, V_ROWS, tq), F32),
                        pltpu.VMEM((SCORE_RING, tq, tq // 2), F32)],
        compiler_params=_params(3),
        name="attn",
    )(qt, k, vt, sg)


def _final_kernel(x_ref, yf_ref, ym_ref, w_ref, mod_ref, fg_ref, o_ref):
    y = _dot(yf_ref[0], w_ref[0:FOX_W, :]) + _dot(ym_ref[0], w_ref[FOX_W:2 * FOX_W, :])
    gate = mod_ref[0, :, 2 * D_MODEL:3 * D_MODEL]
    o_ref[0] = _rms(x_ref[0] + gate * y, fg_ref[...])


def _final_project(x, y, wout, mod4, final_g):
    batch, seq, _ = x.shape
    ts = FINAL_TILE
    row_tile = lambda width: pl.BlockSpec((1, ts, width), lambda b, s: (b, s, 0))
    group_rows = lambda g: pl.BlockSpec((None, 1, ts, FOX_W), lambda b, s: (g, b, s, 0))
    return pl.pallas_call(
        _final_kernel,
        grid=(batch, seq // ts),
        in_specs=[row_tile(D_MODEL), group_rows(0), group_rows(1),
                  _layer_slab(wout, DEPTH - 1), _mod_spec(DEPTH - 1), _resident((1, D_MODEL))],
        out_specs=row_tile(D_MODEL),
        out_shape=jax.ShapeDtypeStruct(x.shape, F32),
        compiler_params=_params(),
        name="out_proj_final",
    )(x, y, y, wout, mod4, final_g.reshape(1, D_MODEL))


def kernel(x, c, positions, norm_g, w_ada, b_ada, w_in, b_f, q_norm_g, w_uq, kv_norm_g, w_ukv,
           w_out, final_g):
    batch = x.shape[0]
    mod4 = _adaln(c, w_ada, b_ada).reshape(DEPTH, batch, 1, 3 * D_MODEL)
    tables = _rope_tables(positions)
    eq, ek = _aug_matrices()
    ltri = np.tril(np.ones((SEQ_TILE, SEQ_TILE), np.float32))
    consts = (jnp.asarray(eq.T, BF16), jnp.asarray(ek, BF16), jnp.asarray(_q_ones_rows()),
              jnp.asarray(ltri, BF16))
    w = _prep_weights(w_in, w_uq, w_ukv, w_out, b_f, q_norm_g, kv_norm_g)
    norm_g3 = norm_g.reshape(DEPTH, 1, D_MODEL)
    prev = None
    for layer in range(DEPTH):
        x, (qt, k, vt, sg) = _project(x, prev, layer, mod4, norm_g3, w, consts, tables)
        prev = _attention(qt, k, vt, sg)
    return _final_project(x, prev, w["wout"], mod4, final_g)
```
